```python
import math
import jax, jax.numpy as jnp
from jax import lax
import numpy as np

D_MODEL = 1024
BATCH = 32
SEQ = 256
DEPTH = 2
DEC_BATCH = 2
DEC_SEQ = 4096
PAST_LEN = 512

GRID_W = 64
D_MIX = D_MODEL
D_POOL = D_MIX // 2
D_LRU = D_MIX - D_POOL
D_IN = D_POOL + 2 * D_LRU
POOL_WINDOWS = (2, 4, 8, 16)
N_POOL_GROUPS = len(POOL_WINDOWS)
POOL_GROUP_W = D_POOL // N_POOL_GROUPS
N_LRU_HEADS = 8
LRU_HEAD_DIM = D_LRU // N_LRU_HEADS
CONV_W = 4
CONV_LEFT = CONV_W // 2
LRU_C = 8.0
N_EXPERTS = 16
EC_FACTOR = 2
D_EXPERT = 1024
N_MOD = 6
EPS = 1e-6

kernel_name = "hybrid_pool_rglru_ec_diffusion_step"


def _rmsnorm(x, g):
    xf = x.astype(jnp.float32)
    y = xf * lax.rsqrt(jnp.mean(xf * xf, axis=-1, keepdims=True) + EPS)
    return (y * g.astype(jnp.float32)).astype(x.dtype)


def _window_mean(u, w, axis):
    u0 = jnp.moveaxis(u, axis, 0)
    L = u0.shape[0]
    uf = u0.astype(jnp.float32)
    cs = jnp.concatenate([jnp.zeros_like(uf[:1]), jnp.cumsum(uf, axis=0)], axis=0)
    t = jnp.arange(L)
    lo = jnp.clip(t - w // 2, 0, L)
    hi = jnp.clip(t + (w - w // 2), 0, L)
    cnt = (hi - lo).astype(jnp.float32).reshape((L,) + (1,) * (u0.ndim - 1))
    m = (cs[hi] - cs[lo]) / cnt
    return jnp.moveaxis(m.astype(u.dtype), 0, axis)


def _pool_mixer(u, rows, w_pool, pool_scale):
    B, L, _ = u.shape
    ug = u.reshape(B, L, N_POOL_GROUPS, POOL_GROUP_W)
    outs = []
    for g, w in enumerate(POOL_WINDOWS):
        x_g = ug[:, :, g, :]
        if rows is None:
            m = _window_mean(x_g, w, 1)
        else:
            grid = x_g.reshape(B, rows, GRID_W, POOL_GROUP_W)
            m = _window_mean(_window_mean(grid, w, 2), w, 1).reshape(B, L, POOL_GROUP_W)
        outs.append(m - x_g)
    d = jnp.stack(outs, axis=2)
    y = jnp.einsum('blgc,gcd->blgd', d, w_pool).reshape(B, L, D_POOL)
    return y * pool_scale


def _conv_centred(x, w_conv, b_conv):
    L = x.shape[1]
    xp = jnp.pad(x, ((0, 0), (CONV_LEFT, CONV_W - 1 - CONV_LEFT), (0, 0)))
    y = b_conv
    for k in range(CONV_W):
        y = y + w_conv[k] * xp[:, k:k + L, :]
    return y


def _lin_combine(e1, e2):
    a1, b1 = e1
    a2, b2 = e2
    return a1 * a2, a2 * b1 + b2


def _rglru(xc, w_a, b_a, w_x, b_x, lam, h0, reverse):
    B, L, C = xc.shape
    xh = xc.reshape(B, L, N_LRU_HEADS, LRU_HEAD_DIM)
    r = jax.nn.sigmoid(jnp.einsum('blhi,hij->blhj', xh, w_a) + b_a).reshape(B, L, C)
    i = jax.nn.sigmoid(jnp.einsum('blhi,hij->blhj', xh, w_x) + b_x).reshape(B, L, C)
    log_a = -LRU_C * r.astype(jnp.float32) * jax.nn.softplus(-lam.astype(jnp.float32))
    a = jnp.exp(log_a)
    mult = jnp.sqrt(jnp.maximum(-jnp.expm1(2.0 * log_a), 0.0))
    b = mult * (i * xc).astype(jnp.float32)
    A, Bc = lax.associative_scan(_lin_combine, (a, b), axis=1, reverse=reverse)
    return A * h0.astype(jnp.float32)[:, None, :] + Bc


def _ec_moe(h, w_router, w_exp_gate, w_exp_up, w_exp_down):
    B, L, D = h.shape
    N = B * L
    flat = h.reshape(N, D)
    aff = jax.nn.softmax(jnp.einsum('nd,de->ne', flat, w_router).astype(jnp.float32), axis=-1)
    cap = EC_FACTOR * N // N_EXPERTS
    g, idx = lax.top_k(aff.T, cap)
    xs = flat[idx]
    hid = jax.nn.silu(jnp.einsum('ecd,edf->ecf', xs, w_exp_gate)) * jnp.einsum('ecd,edf->ecf', xs, w_exp_up)
    out = jnp.einsum('ecf,efd->ecd', hid, w_exp_down) * g[..., None].astype(h.dtype)
    y = jnp.zeros((N, D), h.dtype).at[idx.reshape(-1)].add(out.reshape(-1, D))
    return y.reshape(B, L, D)


def _layer(x, mod, h0, rows, p, want_state):
    shift1, scale1, gate1, shift2, scale2, gate2 = jnp.split(mod, N_MOD, axis=-1)
    hn = _rmsnorm(x, p['g_norm_mix']) * (1 + scale1) + shift1
    proj = jnp.einsum('bld,dk->blk', hn, p['w_in'])
    u_pool = proj[..., :D_POOL]
    u_lru = proj[..., D_POOL:D_POOL + D_LRU]
    u_gate = proj[..., D_POOL + D_LRU:]
    y_pool = _pool_mixer(u_pool, rows, p['w_pool'], p['pool_scale'])
    xc = _conv_centred(u_lru, p['w_conv'], p['b_conv'])
    hf = _rglru(xc, p['w_gate_a'][0], p['b_gate_a'][0], p['w_gate_x'][0], p['b_gate_x'][0], p['lru_lambda'][0], h0[:, 0], False)
    hb = _rglru(xc, p['w_gate_a'][1], p['b_gate_a'][1], p['w_gate_x'][1], p['b_gate_x'][1], p['lru_lambda'][1], h0[:, 1], True)
    y_lru = jax.nn.gelu(u_gate) * (hf + hb).astype(x.dtype)
    mix = jnp.einsum('blk,kd->bld', jnp.concatenate([y_pool, y_lru], axis=-1), p['w_out'])
    x = x + gate1 * mix
    hn2 = _rmsnorm(x, p['g_norm_ffn']) * (1 + scale2) + shift2
    x = x + gate2 * _ec_moe(hn2, p['w_router'], p['w_exp_gate'], p['w_exp_up'], p['w_exp_down'])
    if want_state:
        st = jnp.stack([hf[:, -1], hb[:, 0]], axis=1).astype(x.dtype)
        return x, st
    return x, None


def setup_inputs(seed: int = 0) -> dict:
    key = jax.random.key(seed)
    ks = jax.random.split(key, 32)
    f32 = jnp.float32
    nrm = lambda k, shape, s: jax.random.normal(k, shape, f32) * s
    u = jax.random.uniform(ks[20], (DEPTH, 2, D_LRU), f32, 0.9, 0.999)
    sig = u ** (1.0 / LRU_C)
    lam = jnp.log(sig) - jnp.log1p(-sig)
    return {
        'x_prompt': nrm(ks[0], (BATCH, SEQ, D_MODEL), 1.0),
        'x_sample': nrm(ks[1], (DEC_BATCH, DEC_SEQ, D_MODEL), 1.0),
        'state_rglru': nrm(ks[2], (DEC_BATCH, DEPTH, 2, D_LRU), 0.5),
        'c': nrm(ks[3], (DEC_BATCH, D_MODEL), 1.0),
        'c_ctx': nrm(ks[4], (D_MODEL,), 1.0),
        'w_mod': nrm(ks[5], (DEPTH, D_MODEL, N_MOD * D_MODEL), 0.5 * D_MODEL ** -0.5),
        'b_mod': nrm(ks[6], (DEPTH, N_MOD * D_MODEL), 0.01),
        'g_norm_mix': 1.0 + nrm(ks[7], (DEPTH, D_MODEL), 0.02),
        'g_norm_ffn': 1.0 + nrm(ks[8], (DEPTH, D_MODEL), 0.02),
        'w_in': nrm(ks[9], (DEPTH, D_MODEL, D_IN), D_MODEL ** -0.5),
        'w_pool': nrm(ks[10], (DEPTH, N_POOL_GROUPS, POOL_GROUP_W, POOL_GROUP_W), POOL_GROUP_W ** -0.5),
        'pool_scale': 0.5 + nrm(ks[11], (DEPTH, D_POOL), 0.1),
        'w_conv': nrm(ks[12], (DEPTH, CONV_W, D_LRU), CONV_W ** -0.5),
        'b_conv': nrm(ks[13], (DEPTH, D_LRU), 0.01),
        'w_gate_a': nrm(ks[14], (DEPTH, 2, N_LRU_HEADS, LRU_HEAD_DIM, LRU_HEAD_DIM), LRU_HEAD_DIM ** -0.5),
        'b_gate_a': nrm(ks[15], (DEPTH, 2, N_LRU_HEADS, LRU_HEAD_DIM), 0.01),
        'w_gate_x': nrm(ks[16], (DEPTH, 2, N_LRU_HEADS, LRU_HEAD_DIM, LRU_HEAD_DIM), LRU_HEAD_DIM ** -0.5),
        'b_gate_x': nrm(ks[17], (DEPTH, 2, N_LRU_HEADS, LRU_HEAD_DIM), 0.01),
        'lru_lambda': lam,
        'w_out': nrm(ks[18], (DEPTH, D_MIX, D_MODEL), D_MIX ** -0.5),
        'w_router': nrm(ks[19], (DEPTH, D_MODEL, N_EXPERTS), D_MODEL ** -0.5),
        'w_exp_gate': nrm(ks[21], (DEPTH, N_EXPERTS, D_MODEL, D_EXPERT), D_MODEL ** -0.5),
        'w_exp_up': nrm(ks[22], (DEPTH, N_EXPERTS, D_MODEL, D_EXPERT), D_MODEL ** -0.5),
        'w_exp_down': nrm(ks[23], (DEPTH, N_EXPERTS, D_EXPERT, D_MODEL), D_EXPERT ** -0.5),
        'g_final': 1.0 + nrm(ks[24], (D_MODEL,), 0.02),
    }


def reference(x_prompt, x_sample, state_rglru, c, c_ctx, w_mod, b_mod, g_norm_mix, g_norm_ffn, w_in, w_pool, pool_scale, w_conv, b_conv, w_gate_a, b_gate_a, w_gate_x, b_gate_x, lru_lambda, w_out, w_router, w_exp_gate, w_exp_up, w_exp_down, g_final):
    B = x_prompt.shape[0]
    rows = x_sample.shape[1] // GRID_W
    h0_ctx = jnp.zeros((B, 2, D_LRU), x_prompt.dtype)
    xp = x_prompt
    xs = x_sample
    states = []
    for l in range(DEPTH):
        p = {
            'g_norm_mix': g_norm_mix[l], 'g_norm_ffn': g_norm_ffn[l], 'w_in': w_in[l],
            'w_pool': w_pool[l], 'pool_scale': pool_scale[l], 'w_conv': w_conv[l], 'b_conv': b_conv[l],
            'w_gate_a': w_gate_a[l], 'b_gate_a': b_gate_a[l], 'w_gate_x': w_gate_x[l], 'b_gate_x': b_gate_x[l],
            'lru_lambda': lru_lambda[l], 'w_out': w_out[l], 'w_router': w_router[l],
            'w_exp_gate': w_exp_gate[l], 'w_exp_up': w_exp_up[l], 'w_exp_down': w_exp_down[l],
        }
        mod_ctx = (jnp.einsum('d,dk->k', jax.nn.silu(c_ctx), w_mod[l]) + b_mod[l])[None, None, :]
        mod_lat = (jnp.einsum('bd,dk->bk', jax.nn.silu(c), w_mod[l]) + b_mod[l])[:, None, :]
        xp, st = _layer(xp, mod_ctx, h0_ctx, None, p, True)
        xs, _ = _layer(xs, mod_lat, state_rglru[:, l], rows, p, False)
        states.append(st)
    y_prompt = _rmsnorm(xp, g_final)
    y_sample = _rmsnorm(xs, g_final)
    new_state_rglru = jnp.stack(states, axis=1)
    return (y_prompt, y_sample, new_state_rglru)
```

```python
import functools

import jax
import jax.numpy as jnp
from jax import lax
from jax.experimental import pallas as pl
from jax.experimental.pallas import tpu as pltpu

F32 = jnp.float32
BF16 = jnp.bfloat16
HIGHEST = lax.Precision.HIGHEST

D = 1024
NTOK = 8192
NALL = 2 * NTOK
CTX_L = 256
LAT_L = 4096
GRID_W = 64
DEPTH = 2
D_POOL = 512
D_LRU = 512
D_IN = D_POOL + 2 * D_LRU
POOL_WINDOWS = (2, 4, 8, 16)
LANES = 128
N_EXPERTS = 16
CAP = 2 * NTOK // N_EXPERTS
D_EXPERT = 1024
LRU_C = 8.0
EPS = 1e-6

TM = 512
SEG = 4096
CH = 256
NCH = SEG // CH
TT = 256
NTT = NTOK // TT
VMEM_LIMIT = 56 * 1024 * 1024


def _cp(sem, vmem=VMEM_LIMIT):
    return pltpu.CompilerParams(dimension_semantics=sem, vmem_limit_bytes=vmem)


def _mod_row(i):
    n_ctx = NTOK // TM
    return jnp.where(i < n_ctx, 0, 1 + (i - n_ctx) // (LAT_L // TM))


def _mod_kernel(c_ref, w_ref, b_ref, o_ref):
    cv = c_ref[...]
    s = cv * jax.nn.sigmoid(cv)
    o_ref[0, 0] = jnp.dot(s, w_ref[0], precision=HIGHEST, preferred_element_type=F32) + b_ref[0, 0]


def _modulation(cvec, w_mod, b_mod):
    return pl.pallas_call(
        _mod_kernel,
        grid=(DEPTH, 6),
        in_specs=[
            pl.BlockSpec((8, D), lambda l, j: (0, 0)),
            pl.BlockSpec((1, D, D), lambda l, j: (l, 0, j)),
            pl.BlockSpec((1, 1, 1, D), lambda l, j: (l, j, 0, 0)),
        ],
        out_specs=pl.BlockSpec((1, 1, 8, D), lambda l, j: (l, j, 0, 0)),
        out_shape=jax.ShapeDtypeStruct((DEPTH, 6, 8, D), F32),
        compiler_params=_cp(("arbitrary", "arbitrary")),
        name="modulation",
    )(cvec, w_mod, b_mod.reshape(DEPTH, 6, 1, D))


def _rms(x, g):
    return x * lax.rsqrt(jnp.mean(x * x, axis=-1, keepdims=True) + EPS) * g


def _inproj_kernel(x_ref, mod_ref, g_ref, w_ref, up_ref, ul_ref, ug_ref):
    x = x_ref[...]
    hn = _rms(x, g_ref[...]) * (1.0 + mod_ref[0, 1:2, :]) + mod_ref[0, 0:1, :]
    proj = jnp.dot(hn.astype(BF16), w_ref[...], preferred_element_type=F32)
    up_ref[...] = proj[:, :D_POOL]
    ul_ref[...] = proj[:, D_POOL:D_POOL + D_LRU]
    ug_ref[...] = proj[:, D_POOL + D_LRU:]


def _inproj(x, mod_l, g, w_in_bf):
    out = jax.ShapeDtypeStruct((NALL, D_POOL), F32)
    return pl.pallas_call(
        _inproj_kernel,
        grid=(NALL // TM,),
        in_specs=[
            pl.BlockSpec((TM, D), lambda i: (i, 0)),
            pl.BlockSpec((1, 8, D), lambda i: (_mod_row(i), 0, 0)),
            pl.BlockSpec((1, D), lambda i: (0, 0)),
            pl.BlockSpec((D, D_IN), lambda i: (0, 0)),
        ],
        out_specs=[pl.BlockSpec((TM, D_POOL), lambda i: (i, 0))] * 3,
        out_shape=[out, out, out],
        compiler_params=_cp(("arbitrary",)),
        name="inproj",
    )(x, mod_l, g.reshape(1, D), w_in_bf)


def _window_sum(x, h, pos, period):
    n = x.shape[0]
    acc = x
    for i in range(1, h):
        acc = acc + jnp.where(pos < period - i, pltpu.roll(x, n - i, axis=0), 0.0)
    for i in range(1, h + 1):
        acc = acc + jnp.where(pos >= i, pltpu.roll(x, i, axis=0), 0.0)
    return acc


def _window_count(pos, h, period):
    return (jnp.minimum(pos + h, period) - jnp.maximum(pos - h, 0)).astype(F32)


def _pool_kernel(u_ref, w_ref, s_ref, o_ref, pad_ref):
    seg = pl.program_id(0)
    row = lax.broadcasted_iota(jnp.int32, (CH, LANES), 0)
    halo = 8 * GRID_W

    def project(d, g, t0):
        y = jnp.dot(d.astype(BF16), w_ref[g], preferred_element_type=F32)
        y = y * s_ref[:, g * LANES:(g + 1) * LANES]
        o_ref[pl.ds(t0, CH), g * LANES:(g + 1) * LANES] = y.astype(BF16)

    @pl.when(seg < NTOK // SEG)
    def _context():
        def body(sb, c):
            t0 = pl.multiple_of(sb * CH, CH)
            for g, w in enumerate(POOL_WINDOWS):
                h = w // 2
                x = u_ref[pl.ds(t0, CH), g * LANES:(g + 1) * LANES]
                m = _window_sum(x, h, row, CTX_L) / _window_count(row, h, CTX_L)
                project(m - x, g, t0)
            return c
        lax.fori_loop(0, NCH, body, 0)

    @pl.when(seg >= NTOK // SEG)
    def _latent():
        zeros = jnp.zeros((halo, D_POOL), F32)
        pad_ref[0:halo, :] = zeros
        pad_ref[halo + SEG:halo + SEG + halo, :] = zeros
        col = row & (GRID_W - 1)

        def cols(sb, c):
            t0 = pl.multiple_of(sb * CH, CH)
            for g, w in enumerate(POOL_WINDOWS):
                h = w // 2
                x = u_ref[pl.ds(t0, CH), g * LANES:(g + 1) * LANES]
                m = _window_sum(x, h, col, GRID_W) / _window_count(col, h, GRID_W)
                pad_ref[pl.ds(halo + t0, CH), g * LANES:(g + 1) * LANES] = m
            return c
        lax.fori_loop(0, NCH, cols, 0)

        def rows(sb, c):
            t0 = pl.multiple_of(sb * CH, CH)
            r = jnp.right_shift(t0 + row, 6)
            for g, w in enumerate(POOL_WINDOWS):
                h = w // 2
                acc = None
                for i in range(-h, h):
                    start = pl.multiple_of(halo + t0 + i * GRID_W, GRID_W)
                    v = pad_ref[pl.ds(start, CH), g * LANES:(g + 1) * LANES]
                    acc = v if acc is None else acc + v
                m = acc / _window_count(r, h, GRID_W)
                x = u_ref[pl.ds(t0, CH), g * LANES:(g + 1) * LANES]
                project(m - x, g, t0)
            return c
        lax.fori_loop(0, NCH, rows, 0)


def _pool(u_pool, w_pool_bf, pool_scale):
    halo = 8 * GRID_W
    return pl.pallas_call(
        _pool_kernel,
        grid=(NALL // SEG,),
        in_specs=[
            pl.BlockSpec((SEG, D_POOL), lambda i: (i, 0)),
            pl.BlockSpec((len(POOL_WINDOWS), LANES, LANES), lambda i: (0, 0, 0)),
            pl.BlockSpec((1, D_POOL), lambda i: (0, 0)),
        ],
        out_specs=pl.BlockSpec((SEG, D_POOL), lambda i: (i, 0)),
        out_shape=jax.ShapeDtypeStruct((NALL, D_POOL), BF16),
        scratch_shapes=[pltpu.VMEM((SEG + 2 * halo, D_POOL), F32)],
        compiler_params=_cp(("arbitrary",)),
        name="pool",
    )(u_pool, w_pool_bf, pool_scale.reshape(1, D_POOL))


def _lru_kernel(ul_ref, ug_ref, wc_ref, bc_ref, wg_ref, bg_ref, lam_ref, h0_ref,
                o_ref, st_ref, xc_ref, a_ref, b_ref, hin_ref):
    seg = pl.program_id(0)
    is_lat = seg >= NTOK // SEG
    last = jnp.where(is_lat, LAT_L - 1, CTX_L - 1)

    x = ul_ref[...]
    pos = lax.broadcasted_iota(jnp.int32, (SEG, LANES), 0) & last
    xc = bc_ref[...] + wc_ref[0:1, :] * jnp.where(pos >= 2, pltpu.roll(x, 2, axis=0), 0.0)
    xc = xc + wc_ref[1:2, :] * jnp.where(pos >= 1, pltpu.roll(x, 1, axis=0), 0.0)
    xc = xc + wc_ref[2:3, :] * x
    xc = xc + wc_ref[3:4, :] * jnp.where(pos < last, pltpu.roll(x, SEG - 1, axis=0), 0.0)
    xc_ref[...] = xc

    def gates(c, carry):
        t0 = pl.multiple_of(c * CH, CH)
        xcb = xc_ref[pl.ds(t0, CH), :]
        xb = xcb.astype(BF16)
        for d in range(2):
            r = jax.nn.sigmoid(jnp.dot(xb, wg_ref[0, 2 * d], preferred_element_type=F32) + bg_ref[2 * d:2 * d + 1, :])
            i = jax.nn.sigmoid(jnp.dot(xb, wg_ref[0, 2 * d + 1], preferred_element_type=F32) + bg_ref[2 * d + 1:2 * d + 2, :])
            log_a = (-LRU_C) * r * jax.nn.softplus(-lam_ref[d:d + 1, :])
            a = jnp.exp(log_a)
            mult = jnp.sqrt(jnp.maximum(-jnp.tanh(log_a) * (a * a + 1.0), 0.0))
            a_ref[d, pl.ds(t0, CH), :] = a
            b_ref[d, pl.ds(t0, CH), :] = mult * (i * xcb)
        return carry
    lax.fori_loop(0, NCH, gates, 0)

    def scan(j, carry):
        hf, pf, hb, pb = carry
        jb = CH - 1 - j
        af = a_ref[0, pl.ds(j, NCH, stride=CH), :]
        bf = b_ref[0, pl.ds(j, NCH, stride=CH), :]
        ab = a_ref[1, pl.ds(jb, NCH, stride=CH), :]
        bb = b_ref[1, pl.ds(jb, NCH, stride=CH), :]
        hf = af * hf + bf
        pf = af * pf
        hb = ab * hb + bb
        pb = ab * pb
        b_ref[0, pl.ds(j, NCH, stride=CH), :] = hf
        a_ref[0, pl.ds(j, NCH, stride=CH), :] = pf
        b_ref[1, pl.ds(jb, NCH, stride=CH), :] = hb
        a_ref[1, pl.ds(jb, NCH, stride=CH), :] = pb
        return hf, pf, hb, pb
    z = jnp.zeros((NCH, LANES), F32)
    o = jnp.ones((NCH, LANES), F32)
    hf_end, pf_end, hb_beg, pb_beg = lax.fori_loop(0, CH, scan, (z, o, z, o))

    cur = h0_ref[0, 0:1, :]
    rows_f = []
    for c in range(NCH):
        rows_f.append(cur)
        cur = pf_end[c:c + 1, :] * cur + hf_end[c:c + 1, :]
    cur = h0_ref[0, 1:2, :]
    rows_b = [None] * NCH
    for c in range(NCH - 1, -1, -1):
        rows_b[c] = cur
        cur = pb_beg[c:c + 1, :] * cur + hb_beg[c:c + 1, :]
    hin_f = jnp.where(is_lat, jnp.concatenate(rows_f, axis=0), 0.0)
    hin_b = jnp.where(is_lat, jnp.concatenate(rows_b, axis=0), 0.0)
    hin_ref[0] = hin_f
    hin_ref[1] = hin_b
    st_ref[0, 0] = pf_end * hin_f + hf_end
    st_ref[0, 1] = pb_beg * hin_b + hb_beg

    def finish(c, carry):
        t0 = pl.multiple_of(c * CH, CH)
        hf = a_ref[0, pl.ds(t0, CH), :] * hin_ref[0, pl.ds(c, 1), :] + b_ref[0, pl.ds(t0, CH), :]
        hb = a_ref[1, pl.ds(t0, CH), :] * hin_ref[1, pl.ds(c, 1), :] + b_ref[1, pl.ds(t0, CH), :]
        y = jax.nn.gelu(ug_ref[pl.ds(t0, CH), :]) * (hf + hb)
        o_ref[pl.ds(t0, CH), :] = y.astype(BF16)
        return carry
    lax.fori_loop(0, NCH, finish, 0)


def _lru(u_lru, u_gate, w_conv, b_conv, wg_bd, bg, lam, h0):
    nseg = NALL // SEG
    ncb = D_LRU // LANES
    return pl.pallas_call(
        _lru_kernel,
        grid=(nseg, ncb),
        in_specs=[
            pl.BlockSpec((SEG, LANES), lambda s, c: (s, c)),
            pl.BlockSpec((SEG, LANES), lambda s, c: (s, c)),
            pl.BlockSpec((4, LANES), lambda s, c: (0, c)),
            pl.BlockSpec((1, LANES), lambda s, c: (0, c)),
            pl.BlockSpec((1, 4, LANES, LANES), lambda s, c: (c, 0, 0, 0)),
            pl.BlockSpec((4, LANES), lambda s, c: (0, c)),
            pl.BlockSpec((2, LANES), lambda s, c: (0, c)),
            pl.BlockSpec((1, 2, LANES), lambda s, c: (s, 0, c)),
        ],
        out_specs=[
            pl.BlockSpec((SEG, LANES), lambda s, c: (s, c)),
            pl.BlockSpec((1, 2, NCH, LANES), lambda s, c: (s, 0, 0, c)),
        ],
        out_shape=[
            jax.ShapeDtypeStruct((NALL, D_LRU), BF16),
            jax.ShapeDtypeStruct((nseg, 2, NCH, D_LRU), F32),
        ],
        scratch_shapes=[
            pltpu.VMEM((SEG, LANES), F32),
            pltpu.VMEM((2, SEG, LANES), F32),
            pltpu.VMEM((2, SEG, LANES), F32),
            pltpu.VMEM((2, NCH, LANES), F32),
        ],
        compiler_params=_cp(("arbitrary", "arbitrary")),
        name="lru",
    )(u_lru, u_gate, w_conv, b_conv.reshape(1, D_LRU), wg_bd, bg, lam, h0)


def _outproj_kernel(x_ref, yp_ref, yl_ref, mod_ref, g_ref, wo_ref, wr_ref, x1_ref, hn_ref, aff_ref):
    mix = jnp.dot(yp_ref[...], wo_ref[0:D_POOL, :], preferred_element_type=F32)
    mix = mix + jnp.dot(yl_ref[...], wo_ref[D_POOL:, :], preferred_element_type=F32)
    x1 = x_ref[...] + mod_ref[0, 2:3, :] * mix
    x1_ref[...] = x1
    hn = _rms(x1, g_ref[...]) * (1.0 + mod_ref[0, 4:5, :]) + mod_ref[0, 3:4, :]
    hn_ref[...] = hn
    logits = lax.dot_general(wr_ref[...], hn, (((1,), (1,)), ((), ())),
                             precision=HIGHEST, preferred_element_type=F32)
    e = jnp.exp(logits - jnp.max(logits, axis=0, keepdims=True))
    aff_ref[...] = e / jnp.sum(e, axis=0, keepdims=True)


def _outproj(x, y_pool, y_lru, mod_l, g, w_out_bf, w_router_t):
    return pl.pallas_call(
        _outproj_kernel,
        grid=(NALL // TM,),
        in_specs=[
            pl.BlockSpec((TM, D), lambda i: (i, 0)),
            pl.BlockSpec((TM, D_POOL), lambda i: (i, 0)),
            pl.BlockSpec((TM, D_LRU), lambda i: (i, 0)),
            pl.BlockSpec((1, 8, D), lambda i: (_mod_row(i), 0, 0)),
            pl.BlockSpec((1, D), lambda i: (0, 0)),
            pl.BlockSpec((D, D), lambda i: (0, 0)),
            pl.BlockSpec((N_EXPERTS, D), lambda i: (0, 0)),
        ],
        out_specs=[
            pl.BlockSpec((TM, D), lambda i: (i, 0)),
            pl.BlockSpec((TM, D), lambda i: (i, 0)),
            pl.BlockSpec((N_EXPERTS, TM), lambda i: (0, i)),
        ],
        out_shape=[
            jax.ShapeDtypeStruct((NALL, D), F32),
            jax.ShapeDtypeStruct((NALL, D), F32),
            jax.ShapeDtypeStruct((N_EXPERTS, NALL), F32),
        ],
        compiler_params=_cp(("arbitrary",)),
        name="outproj",
    )(x, y_pool, y_lru, mod_l, g.reshape(1, D), w_out_bf, w_router_t)


def _lane_prefix(m, tri):
    parts = []
    bounds = []
    run = jnp.zeros((m.shape[0], 1), F32)
    for k in range(NTOK // LANES):
        if (k * LANES) % TT == 0:
            bounds.append(run)
        p = jnp.dot(m[:, k * LANES:(k + 1) * LANES].astype(BF16), tri, preferred_element_type=F32)
        parts.append(p + run)
        run = run + p[:, LANES - 1:LANES]
    bounds.append(run)
    return jnp.concatenate(parts, axis=1), bounds


def _plan_kernel(aff_ref, slot_ref, kord_ref, cnt_ref, off_ref, maxk_ref):
    aff = aff_ref[...]
    bits = pltpu.bitcast(aff, jnp.int32)

    def count_ge(v):
        return jnp.sum(jnp.where(bits >= v, 1.0, 0.0), axis=1, keepdims=True)

    def bisect(_, c):
        lo, hi = c
        mid = lo + jnp.right_shift(hi - lo + 1, 1)
        ok = count_ge(mid) >= float(CAP)
        return jnp.where(ok, mid, lo), jnp.where(ok, hi, mid - 1)
    lo0 = jnp.zeros((N_EXPERTS, 1), jnp.int32)
    hi0 = jnp.full((N_EXPERTS, 1), 0x7F7FFFFF, jnp.int32)
    thr, _ = lax.fori_loop(0, 31, bisect, (lo0, hi0))

    ri = lax.broadcasted_iota(jnp.int32, (LANES, LANES), 0)
    ci = lax.broadcasted_iota(jnp.int32, (LANES, LANES), 1)
    tri = jnp.where(ri <= ci, 1.0, 0.0).astype(BF16)

    gt = bits > thr
    eq = bits == thr
    need = float(CAP) - jnp.sum(jnp.where(gt, 1.0, 0.0), axis=1, keepdims=True)
    eq_rank, _ = _lane_prefix(jnp.where(eq, 1.0, 0.0), tri)
    sel = jnp.where(gt, 1.0, jnp.where(eq & (eq_rank <= need), 1.0, 0.0))
    rank, bounds = _lane_prefix(sel, tri)
    slot_ref[0] = jnp.where(sel > 0.0, rank - 1.0, -1.0)

    er = lax.broadcasted_iota(jnp.int32, (N_EXPERTS, N_EXPERTS), 0)
    ec = lax.broadcasted_iota(jnp.int32, (N_EXPERTS, N_EXPERTS), 1)
    below = jnp.where(ec < er, 1.0, 0.0).astype(BF16)
    kord_ref[0] = jnp.dot(below, sel.astype(BF16), preferred_element_type=F32)
    cnt = jnp.sum(sel, axis=0, keepdims=True)
    cnt_ref[0] = cnt

    lane = lax.broadcasted_iota(jnp.int32, (N_EXPERTS, LANES), 1)
    offs = jnp.zeros((N_EXPERTS, LANES), F32)
    for b, col in enumerate(bounds):
        offs = jnp.where(lane == b, col, offs)
    off_ref[0] = offs.astype(jnp.int32)

    lane1 = lax.broadcasted_iota(jnp.int32, (1, LANES), 1)
    mk = jnp.zeros((1, LANES), F32)
    for b in range(NTT):
        mk = jnp.where(lane1 == b, jnp.max(cnt[:, b * TT:(b + 1) * TT], axis=1, keepdims=True), mk)
    maxk_ref[0] = mk.astype(jnp.int32)


def _plan(aff_t):
    ns = NALL // NTOK
    return pl.pallas_call(
        _plan_kernel,
        grid=(ns,),
        in_specs=[pl.BlockSpec((N_EXPERTS, NTOK), lambda s: (0, s))],
        out_specs=[
            pl.BlockSpec((1, N_EXPERTS, NTOK), lambda s: (s, 0, 0)),
            pl.BlockSpec((1, N_EXPERTS, NTOK), lambda s: (s, 0, 0)),
            pl.BlockSpec((1, 1, NTOK), lambda s: (s, 0, 0)),
            pl.BlockSpec((1, N_EXPERTS, LANES), lambda s: (s, 0, 0)),
            pl.BlockSpec((1, 1, LANES), lambda s: (s, 0, 0)),
        ],
        out_shape=[
            jax.ShapeDtypeStruct((ns, N_EXPERTS, NTOK), F32),
            jax.ShapeDtypeStruct((ns, N_EXPERTS, NTOK), F32),
            jax.ShapeDtypeStruct((ns, 1, NTOK), F32),
            jax.ShapeDtypeStruct((ns, N_EXPERTS, LANES), jnp.int32),
            jax.ShapeDtypeStruct((ns, 1, LANES), jnp.int32),
        ],
        compiler_params=_cp(("arbitrary",)),
        name="plan",
    )(aff_t)


def _compact_kernel(off_ref, slot_ref, aff_ref, kord_ref, o_ref, acc_ref):
    p = pl.program_id(0)
    acc_ref[...] = jnp.zeros_like(acc_ref)
    sub = lax.broadcasted_iota(jnp.int32, (TT, TT), 0).astype(F32)
    lane = lax.broadcasted_iota(jnp.int32, (1, TT), 1).astype(F32)

    def tile(b, c):
        lo = off_ref[p * LANES + b]
        hi = off_ref[p * LANES + b + 1]

        @pl.when(hi > lo)
        def _():
            t0 = pl.multiple_of(b * TT, TT)
            srow = slot_ref[0, :, pl.ds(t0, TT)]
            g = aff_ref[0, :, pl.ds(t0, TT)]
            g_hi = g.astype(BF16)
            r1 = g - g_hi.astype(F32)
            g_mid = r1.astype(BF16)
            g_lo = (r1 - g_mid.astype(F32)).astype(BF16)
            tile_id = jnp.full((1, TT), b, jnp.int32).astype(F32)
            data = jnp.concatenate([
                tile_id.astype(BF16), lane.astype(BF16), g_hi, g_mid, g_lo,
                kord_ref[0, :, pl.ds(t0, TT)].astype(BF16),
                jnp.zeros((2, TT), BF16)], axis=0)
            j0 = lo // TT
            for jj in range(2):
                j = j0 + jj

                @pl.when(jnp.logical_and(j < CAP // TT, hi > j * TT))
                def _():
                    onehot = jnp.where(srow - (j * TT).astype(F32) == sub, 1.0, 0.0).astype(BF16)
                    acc_ref[j] += lax.dot_general(data, onehot, (((1,), (1,)), ((), ())),
                                                  preferred_element_type=F32)
        return c
    lax.fori_loop(0, NTT, tile, 0)
    for j in range(CAP // TT):
        o_ref[0, :, j * TT:(j + 1) * TT] = acc_ref[j]


def _compact(offs, slot, aff_rows, kord):
    npair = (NALL // NTOK) * N_EXPERTS
    spec = pl.BlockSpec((1, 1, NTOK), lambda p, off: (p, 0, 0))
    return pl.pallas_call(
        _compact_kernel,
        grid_spec=pltpu.PrefetchScalarGridSpec(
            num_scalar_prefetch=1,
            grid=(npair,),
            in_specs=[spec, spec, spec],
            out_specs=pl.BlockSpec((1, 8, CAP), lambda p, off: (p, 0, 0)),
            scratch_shapes=[pltpu.VMEM((CAP // TT, 8, TT), F32)],
        ),
        out_shape=jax.ShapeDtypeStruct((npair, 8, CAP), F32),
        compiler_params=_cp(("arbitrary",)),
        name="compact",
    )(offs.reshape(-1), slot.reshape(npair, 1, NTOK), aff_rows.reshape(npair, 1, NTOK),
      kord.reshape(npair, 1, NTOK))


GCHUNK = 1024


def _gather_kernel(idx_ref, hn_ref, xs_ref, sem):
    nchunk = idx_ref.shape[0] // GCHUNK

    def wait_chunk():
        pltpu.make_async_copy(hn_ref.at[pl.ds(0, GCHUNK), :], xs_ref.at[pl.ds(0, GCHUNK), :], sem).wait()

    def chunk(c, carry):
        def issue(i, carry2):
            for u in range(8):
                s = c * GCHUNK + i * 8 + u
                pltpu.make_async_copy(hn_ref.at[pl.ds(idx_ref[s], 1), :], xs_ref.at[pl.ds(s, 1), :], sem).start()
            return carry2
        lax.fori_loop(0, GCHUNK // 8, issue, 0)

        @pl.when(c > 0)
        def _():
            wait_chunk()
        return carry
    lax.fori_loop(0, nchunk, chunk, 0)
    wait_chunk()


def _gather(idx_rows, hn):
    n = idx_rows.shape[0]
    return pl.pallas_call(
        _gather_kernel,
        grid_spec=pltpu.PrefetchScalarGridSpec(
            num_scalar_prefetch=1,
            grid=(1,),
            in_specs=[pl.BlockSpec(memory_space=pl.ANY)],
            out_specs=pl.BlockSpec(memory_space=pl.ANY),
            scratch_shapes=[pltpu.SemaphoreType.DMA(())],
        ),
        out_shape=jax.ShapeDtypeStruct((n, D), F32),
        compiler_params=_cp(("arbitrary",)),
        name="gather",
    )(idx_rows, hn)


FSPLIT = 2


def _ffn_kernel(xs_ref, wg_ref, wu_ref, wd_ref, g_ref, o_ref):
    f = pl.program_id(1)
    x = xs_ref[...].astype(BF16)
    hg = jnp.dot(x, wg_ref[0].astype(BF16), preferred_element_type=F32)
    hu = jnp.dot(x, wu_ref[0].astype(BF16), preferred_element_type=F32)
    hid = (hg * jax.nn.sigmoid(hg) * hu).astype(BF16)
    o = jnp.dot(hid, wd_ref[0].astype(BF16), preferred_element_type=F32)

    @pl.when(f == 0)
    def _():
        o_ref[...] = o

    @pl.when(f == FSPLIT - 1)
    def _():
        o_ref[...] = (o_ref[...] + o) * g_ref[0][:, 0:1]


def _ffn(xs, w_gate, w_up, w_down, g_cols):
    npair = xs.shape[0] // CAP
    fw = D_EXPERT // FSPLIT
    return pl.pallas_call(
        _ffn_kernel,
        grid=(npair, FSPLIT),
        in_specs=[
            pl.BlockSpec((CAP, D), lambda p, f: (p, 0)),
            pl.BlockSpec((1, D, fw), lambda p, f: (p % N_EXPERTS, 0, f)),
            pl.BlockSpec((1, D, fw), lambda p, f: (p % N_EXPERTS, 0, f)),
            pl.BlockSpec((1, fw, D), lambda p, f: (p % N_EXPERTS, f, 0)),
            pl.BlockSpec((1, CAP, LANES), lambda p, f: (p, 0, 0)),
        ],
        out_specs=pl.BlockSpec((CAP, D), lambda p, f: (p, 0)),
        out_shape=jax.ShapeDtypeStruct((npair * CAP, D), F32),
        compiler_params=_cp(("arbitrary", "arbitrary")),
        name="ffn",
    )(xs, w_gate, w_up, w_down, g_cols)


def _combine_kernel(off_ref, tok_ref, kord_ref, maxk_ref, x1_ref, mod_ref, cnt_ref, gf_ref, eo_ref,
                    o_ref, stage_ref, sem, *, final):
    i = pl.program_id(0)
    s = i // NTT
    b = i % NTT

    total = 0
    for e in range(N_EXPERTS):
        pe = s * N_EXPERTS + e
        lo = off_ref[pe * LANES + b]
        hi = off_ref[pe * LANES + b + 1]

        def issue(sl, c, pe=pe):
            r = pe * CAP + sl
            pltpu.make_async_copy(eo_ref.at[pl.ds(r, 1), :],
                                  stage_ref.at[kord_ref[r], pl.ds(tok_ref[r] - b * TT, 1), :], sem).start()
            return c
        lax.fori_loop(lo, hi, issue, 0)
        total = total + (hi - lo)

    def wait_one(_, c):
        pltpu.make_async_copy(eo_ref.at[pl.ds(0, 1), :], stage_ref.at[0, pl.ds(0, 1), :], sem).wait()
        return c
    lax.fori_loop(0, total, wait_one, 0)

    cnt = cnt_ref[:, 0:1]
    o_ref[...] = jnp.zeros_like(o_ref)

    def add(k, c):
        o_ref[...] += jnp.where(cnt > k.astype(F32), stage_ref[k], 0.0)
        return c
    lax.fori_loop(0, maxk_ref[s * LANES + b], add, 0)
    x2 = x1_ref[...] + mod_ref[0, 5:6, :] * o_ref[...]
    if final:
        x2 = _rms(x2, gf_ref[...])
    o_ref[...] = x2


def _combine(offs, tok_local, kord_i, maxk, x1, mod_l, cnt_cols, g_final, eo, final):
    def mrow(i):
        return jnp.where(i < NTT, 0, 1 + (i - NTT) // (LAT_L // TT))
    return pl.pallas_call(
        functools.partial(_combine_kernel, final=final),
        grid_spec=pltpu.PrefetchScalarGridSpec(
            num_scalar_prefetch=4,
            grid=(NALL // TT,),
            in_specs=[
                pl.BlockSpec((TT, D), lambda i, *_: (i, 0)),
                pl.BlockSpec((1, 8, D), lambda i, *_: (mrow(i), 0, 0)),
                pl.BlockSpec((TT, LANES), lambda i, *_: (i, 0)),
                pl.BlockSpec((1, D), lambda i, *_: (0, 0)),
                pl.BlockSpec(memory_space=pl.ANY),
            ],
            out_specs=pl.BlockSpec((TT, D), lambda i, *_: (i, 0)),
            scratch_shapes=[pltpu.VMEM((N_EXPERTS, TT, D), F32), pltpu.SemaphoreType.DMA(())],
        ),
        out_shape=jax.ShapeDtypeStruct((NALL, D), F32),
        compiler_params=_cp(("arbitrary",)),
        name="combine",
    )(offs.reshape(-1), tok_local, kord_i, maxk.reshape(-1), x1, mod_l, cnt_cols, g_final.reshape(1, D), eo)


def _gate_blocks(w_a, w_x):
    hd = w_a.shape[-1]
    per = LANES // hd
    ncb = D_LRU // LANES
    mats = jnp.stack([w_a[0], w_x[0], w_a[1], w_x[1]], axis=0)
    mats = mats.reshape(4, ncb, per, hd, hd)
    out = jnp.zeros((ncb, 4, LANES, LANES), F32)
    for q in range(per):
        out = out.at[:, :, q * hd:(q + 1) * hd, q * hd:(q + 1) * hd].set(mats[:, :, q].transpose(1, 0, 2, 3))
    return out.astype(BF16)


def kernel(x_prompt, x_sample, state_rglru, c, c_ctx, w_mod, b_mod, g_norm_mix, g_norm_ffn, w_in, w_pool,
           pool_scale, w_conv, b_conv, w_gate_a, b_gate_a, w_gate_x, b_gate_x, lru_lambda, w_out, w_router,
           w_exp_gate, w_exp_up, w_exp_down, g_final):
    nb_ctx = x_prompt.shape[0]
    nb_lat = x_sample.shape[0]
    x = jnp.concatenate([x_prompt.reshape(NTOK, D), x_sample.reshape(NTOK, D)], axis=0)

    cvec = jnp.zeros((8, D), F32).at[0].set(c_ctx).at[1:1 + nb_lat].set(c)
    mod = _modulation(cvec, w_mod, b_mod)
    mod = jnp.pad(mod.transpose(0, 2, 1, 3), ((0, 0), (0, 0), (0, 2), (0, 0)))

    nseg = NALL // SEG
    states = []
    for l in range(DEPTH):
        u_pool, u_lru, u_gate = _inproj(x, mod[l], g_norm_mix[l], w_in[l].astype(BF16))
        y_pool = _pool(u_pool, w_pool[l].astype(BF16), pool_scale[l])
        h0 = jnp.zeros((nseg, 2, D_LRU), F32).at[NTOK // SEG:].set(state_rglru[:, l])
        bg = jnp.stack([b_gate_a[l, 0], b_gate_x[l, 0], b_gate_a[l, 1], b_gate_x[l, 1]], axis=0).reshape(4, D_LRU)
        y_lru, st = _lru(u_lru, u_gate, w_conv[l], b_conv[l], _gate_blocks(w_gate_a[l], w_gate_x[l]), bg,
                         lru_lambda[l], h0)
        states.append(st[:NTOK // SEG].transpose(0, 2, 1, 3).reshape(nb_ctx, 2, D_LRU))

        x1, hn, aff_t = _outproj(x, y_pool, y_lru, mod[l], g_norm_ffn[l], w_out[l].astype(BF16), w_router[l].T)
        slot, kord, cnt, offs, maxk = _plan(aff_t)
        aff_rows = aff_t.reshape(N_EXPERTS, NALL // NTOK, NTOK).transpose(1, 0, 2)
        lists = _compact(offs, slot, aff_rows, kord)
        tok_local = (lists[:, 0] * TT + lists[:, 1]).astype(jnp.int32)
        g_sel = lists[:, 2] + lists[:, 3] + lists[:, 4]
        kord_i = lists[:, 5].astype(jnp.int32)
        stream_base = (jnp.arange(tok_local.shape[0], dtype=jnp.int32) // N_EXPERTS) * NTOK
        xs = _gather((tok_local + stream_base[:, None]).reshape(-1), hn)
        g_cols = jnp.broadcast_to(g_sel[:, :, None], g_sel.shape + (LANES,))
        eo = _ffn(xs, w_exp_gate[l], w_exp_up[l], w_exp_down[l], g_cols)
        cnt_cols = jnp.broadcast_to(cnt.reshape(NALL, 1), (NALL, LANES))
        x = _combine(offs, tok_local.reshape(-1), kord_i.reshape(-1), maxk, x1, mod[l], cnt_cols, g_final, eo,
                     final=(l == DEPTH - 1))

    y_prompt = x[:NTOK].reshape(x_prompt.shape)
    y_sample = x[NTOK:].reshape(x_sample.shape)
    return y_prompt, y_sample, jnp.stack(states, axis=1)
```

```python
import functools

import jax
import jax.numpy as jnp
from jax import lax
from jax.experimental import pallas as pl
from jax.experimental.pallas import tpu as pltpu

F32 = jnp.float32
BF16 = jnp.float32
HIGHEST = lax.Precision.HIGHEST

D = 1024
NTOK = 8192
NALL = 2 * NTOK
CTX_L = 256
LAT_L = 4096
GRID_W = 64
DEPTH = 2
D_POOL = 512
D_LRU = 512
D_IN = D_POOL + 2 * D_LRU
POOL_WINDOWS = (2, 4, 8, 16)
LANES = 128
N_EXPERTS = 16
CAP = 2 * NTOK // N_EXPERTS
D_EXPERT = 1024
LRU_C = 8.0
EPS = 1e-6

TM = 512
SEG = 4096
CH = 256
NCH = SEG // CH
CHS = CH + 8
TT = 256
NTT = NTOK // TT
VMEM_LIMIT = 56 * 1024 * 1024


def _cp(sem, vmem=VMEM_LIMIT):
    return pltpu.CompilerParams(dimension_semantics=sem, vmem_limit_bytes=vmem)


def _sigmoid(x):
    return 0.5 * jnp.tanh(0.5 * x) + 0.5


def _mod_row(i):
    n_ctx = NTOK // TM
    return jnp.where(i < n_ctx, 0, 1 + (i - n_ctx) // (LAT_L // TM))


def _mod_kernel(c_ref, w_ref, b_ref, o_ref):
    cv = c_ref[...]
    s = cv * jax.nn.sigmoid(cv)
    o_ref[0, 0] = jnp.dot(s, w_ref[0], precision=HIGHEST, preferred_element_type=F32) + b_ref[0, 0]


def _modulation(cvec, w_mod, b_mod):
    return pl.pallas_call(
        _mod_kernel,
        grid=(DEPTH, 6),
        in_specs=[
            pl.BlockSpec((8, D), lambda l, j: (0, 0)),
            pl.BlockSpec((1, D, D), lambda l, j: (l, 0, j)),
            pl.BlockSpec((1, 1, 1, D), lambda l, j: (l, j, 0, 0)),
        ],
        out_specs=pl.BlockSpec((1, 1, 8, D), lambda l, j: (l, j, 0, 0)),
        out_shape=jax.ShapeDtypeStruct((DEPTH, 6, 8, D), F32),
        compiler_params=_cp(("arbitrary", "arbitrary")),
        name="modulation",
    )(cvec, w_mod, b_mod.reshape(DEPTH, 6, 1, D))


def _rms(x, g):
    return x * lax.rsqrt(jnp.mean(x * x, axis=-1, keepdims=True) + EPS) * g


def _stream_specs(tile):
    n = NTOK // tile
    return [pl.BlockSpec((tile, D), lambda i, *_: (jnp.minimum(i, n - 1), 0)),
            pl.BlockSpec((tile, D), lambda i, *_: (jnp.maximum(i - n, 0), 0))]


def _stream_tile(xa_ref, xb_ref, tile):
    return jnp.where(pl.program_id(0) < NTOK // tile, xa_ref[...], xb_ref[...])


def _inproj_kernel(xa_ref, xb_ref, mod_ref, g_ref, w_ref, up_ref, ul_ref, ug_ref):
    x = _stream_tile(xa_ref, xb_ref, TM)
    hn = _rms(x, g_ref[...]) * (1.0 + mod_ref[0, 1:2, :]) + mod_ref[0, 0:1, :]
    proj = jnp.dot(hn.astype(BF16), w_ref[...], preferred_element_type=F32)
    up_ref[...] = proj[:, :D_POOL]
    ul_ref[...] = proj[:, D_POOL:D_POOL + D_LRU]
    ug_ref[...] = proj[:, D_POOL + D_LRU:]


def _inproj(xa, xb, mod_l, g, w_in_bf):
    out = jax.ShapeDtypeStruct((NALL, D_POOL), F32)
    return pl.pallas_call(
        _inproj_kernel,
        grid=(NALL // TM,),
        in_specs=_stream_specs(TM) + [
            pl.BlockSpec((1, 8, D), lambda i: (_mod_row(i), 0, 0)),
            pl.BlockSpec((1, D), lambda i: (0, 0)),
            pl.BlockSpec((D, D_IN), lambda i: (0, 0)),
        ],
        out_specs=[pl.BlockSpec((TM, D_POOL), lambda i: (i, 0))] * 3,
        out_shape=[out, out, out],
        compiler_params=_cp(("arbitrary",)),
        name="inproj",
    )(xa, xb, mod_l, g.reshape(1, D), w_in_bf)


def _window_sum(x, h, pos, period):
    n = x.shape[0]
    acc = x
    for i in range(1, h):
        acc = acc + jnp.where(pos < period - i, pltpu.roll(x, n - i, axis=0), 0.0)
    for i in range(1, h + 1):
        acc = acc + jnp.where(pos >= i, pltpu.roll(x, i, axis=0), 0.0)
    return acc


def _window_count(pos, h, period):
    return (jnp.minimum(pos + h, period) - jnp.maximum(pos - h, 0)).astype(F32)


def _pool_kernel(u_ref, w_ref, s_ref, o_ref, pad_ref):
    seg = pl.program_id(0)
    row = lax.broadcasted_iota(jnp.int32, (CH, LANES), 0)
    halo = 8 * GRID_W

    def project(d, g, t0):
        y = jnp.dot(d.astype(BF16), w_ref[g], preferred_element_type=F32)
        y = y * s_ref[:, g * LANES:(g + 1) * LANES]
        o_ref[pl.ds(t0, CH), g * LANES:(g + 1) * LANES] = y.astype(BF16)

    @pl.when(seg < NTOK // SEG)
    def _context():
        def body(sb, c):
            t0 = pl.multiple_of(sb * CH, CH)
            for g, w in enumerate(POOL_WINDOWS):
                h = w // 2
                x = u_ref[pl.ds(t0, CH), g * LANES:(g + 1) * LANES]
                m = _window_sum(x, h, row, CTX_L) / _window_count(row, h, CTX_L)
                project(m - x, g, t0)
            return c
        lax.fori_loop(0, NCH, body, 0)

    @pl.when(seg >= NTOK // SEG)
    def _latent():
        zeros = jnp.zeros((halo, D_POOL), F32)
        pad_ref[0:halo, :] = zeros
        pad_ref[halo + SEG:halo + SEG + halo, :] = zeros
        col = row & (GRID_W - 1)

        def cols(sb, c):
            t0 = pl.multiple_of(sb * CH, CH)
            for g, w in enumerate(POOL_WINDOWS):
                h = w // 2
                x = u_ref[pl.ds(t0, CH), g * LANES:(g + 1) * LANES]
                m = _window_sum(x, h, col, GRID_W) / _window_count(col, h, GRID_W)
                pad_ref[pl.ds(halo + t0, CH), g * LANES:(g + 1) * LANES] = m
            return c
        lax.fori_loop(0, NCH, cols, 0)

        def rows(sb, c):
            t0 = pl.multiple_of(sb * CH, CH)
            r = jnp.right_shift(t0 + row, 6)
            for g, w in enumerate(POOL_WINDOWS):
                h = w // 2
                acc = None
                for i in range(-h, h):
                    start = pl.multiple_of(halo + t0 + i * GRID_W, GRID_W)
                    v = pad_ref[pl.ds(start, CH), g * LANES:(g + 1) * LANES]
                    acc = v if acc is None else acc + v
                m = acc / _window_count(r, h, GRID_W)
                x = u_ref[pl.ds(t0, CH), g * LANES:(g + 1) * LANES]
                project(m - x, g, t0)
            return c
        lax.fori_loop(0, NCH, rows, 0)


def _pool(u_pool, w_pool_bf, pool_scale):
    halo = 8 * GRID_W
    return pl.pallas_call(
        _pool_kernel,
        grid=(NALL // SEG,),
        in_specs=[
            pl.BlockSpec((SEG, D_POOL), lambda i: (i, 0)),
            pl.BlockSpec((len(POOL_WINDOWS), LANES, LANES), lambda i: (0, 0, 0)),
            pl.BlockSpec((1, D_POOL), lambda i: (0, 0)),
        ],
        out_specs=pl.BlockSpec((SEG, D_POOL), lambda i: (i, 0)),
        out_shape=jax.ShapeDtypeStruct((NALL, D_POOL), BF16),
        scratch_shapes=[pltpu.VMEM((SEG + 2 * halo, D_POOL), F32)],
        compiler_params=_cp(("arbitrary",)),
        name="pool",
    )(u_pool, w_pool_bf, pool_scale.reshape(1, D_POOL))


def _lru_kernel(ul_ref, ug_ref, wc_ref, bc_ref, wg_ref, bg_ref, lam_ref, h0_ref,
                o_ref, st_ref, xc_ref, a_ref, b_ref, hin_ref):
    seg = pl.program_id(0)
    is_lat = seg >= NTOK // SEG
    last = jnp.where(is_lat, LAT_L - 1, CTX_L - 1)

    x = ul_ref[...]
    pos = lax.broadcasted_iota(jnp.int32, (SEG, LANES), 0) & last
    xc = bc_ref[...] + wc_ref[0:1, :] * jnp.where(pos >= 2, pltpu.roll(x, 2, axis=0), 0.0)
    xc = xc + wc_ref[1:2, :] * jnp.where(pos >= 1, pltpu.roll(x, 1, axis=0), 0.0)
    xc = xc + wc_ref[2:3, :] * x
    xc = xc + wc_ref[3:4, :] * jnp.where(pos < last, pltpu.roll(x, SEG - 1, axis=0), 0.0)
    xc_ref[...] = xc

    def gates(c, carry):
        t0 = pl.multiple_of(c * CH, CH)
        xcb = xc_ref[pl.ds(t0, CH), :]
        xb = xcb.astype(BF16)
        for d in range(2):
            r = _sigmoid(jnp.dot(xb, wg_ref[0, 2 * d], preferred_element_type=F32) + bg_ref[2 * d:2 * d + 1, :])
            i = _sigmoid(jnp.dot(xb, wg_ref[0, 2 * d + 1], preferred_element_type=F32) + bg_ref[2 * d + 1:2 * d + 2, :])
            log_a = (-LRU_C) * r * jax.nn.softplus(-lam_ref[d:d + 1, :])
            a = jnp.exp(log_a)
            m2 = jnp.maximum(-jnp.tanh(log_a) * (a * a + 1.0), 0.0)
            mult = jnp.where(m2 > 0.0, m2 * lax.rsqrt(m2), 0.0)
            s0 = pl.multiple_of(c * CHS, 8)
            a_ref[d, pl.ds(s0, CH), :] = a
            b_ref[d, pl.ds(s0, CH), :] = mult * (i * xcb)
        return carry
    lax.fori_loop(0, NCH, gates, 0)

    def scan(j, carry):
        hf, pf, hb, pb = carry
        jb = CH - 1 - j
        sf = pl.ds(j, NCH, stride=CHS)
        sb = pl.ds(jb, NCH, stride=CHS)
        af = a_ref[0, sf, :]
        bf = b_ref[0, sf, :]
        ab = a_ref[1, sb, :]
        bb = b_ref[1, sb, :]
        hf = af * hf + bf
        pf = af * pf
        hb = ab * hb + bb
        pb = ab * pb
        b_ref[0, sf, :] = hf
        a_ref[0, sf, :] = pf
        b_ref[1, sb, :] = hb
        a_ref[1, sb, :] = pb
        return hf, pf, hb, pb
    z = jnp.zeros((NCH, LANES), F32)
    o = jnp.ones((NCH, LANES), F32)
    hf_end, pf_end, hb_beg, pb_beg = lax.fori_loop(0, CH, scan, (z, o, z, o))

    cur = h0_ref[0, 0:1, :]
    rows_f = []
    for c in range(NCH):
        rows_f.append(cur)
        cur = pf_end[c:c + 1, :] * cur + hf_end[c:c + 1, :]
    cur = h0_ref[0, 1:2, :]
    rows_b = [None] * NCH
    for c in range(NCH - 1, -1, -1):
        rows_b[c] = cur
        cur = pb_beg[c:c + 1, :] * cur + hb_beg[c:c + 1, :]
    hin_f = jnp.where(is_lat, jnp.concatenate(rows_f, axis=0), 0.0)
    hin_b = jnp.where(is_lat, jnp.concatenate(rows_b, axis=0), 0.0)
    hin_ref[0] = hin_f
    hin_ref[1] = hin_b
    st_ref[0, 0] = pf_end * hin_f + hf_end
    st_ref[0, 1] = pb_beg * hin_b + hb_beg

    def finish(c, carry):
        t0 = pl.multiple_of(c * CH, CH)
        rows = pl.ds(pl.multiple_of(c * CHS, 8), CH)
        hf = a_ref[0, rows, :] * hin_ref[0, pl.ds(c, 1), :] + b_ref[0, rows, :]
        hb = a_ref[1, rows, :] * hin_ref[1, pl.ds(c, 1), :] + b_ref[1, rows, :]
        y = jax.nn.gelu(ug_ref[pl.ds(t0, CH), :]) * (hf + hb)
        o_ref[pl.ds(t0, CH), :] = y.astype(BF16)
        return carry
    lax.fori_loop(0, NCH, finish, 0)


def _lru(u_lru, u_gate, w_conv, b_conv, wg_bd, bg, lam, h0):
    nseg = NALL // SEG
    ncb = D_LRU // LANES
    return pl.pallas_call(
        _lru_kernel,
        grid=(nseg, ncb),
        in_specs=[
            pl.BlockSpec((SEG, LANES), lambda s, c: (s, c)),
            pl.BlockSpec((SEG, LANES), lambda s, c: (s, c)),
            pl.BlockSpec((4, LANES), lambda s, c: (0, c)),
            pl.BlockSpec((1, LANES), lambda s, c: (0, c)),
            pl.BlockSpec((1, 4, LANES, LANES), lambda s, c: (c, 0, 0, 0)),
            pl.BlockSpec((4, LANES), lambda s, c: (0, c)),
            pl.BlockSpec((2, LANES), lambda s, c: (0, c)),
            pl.BlockSpec((1, 2, LANES), lambda s, c: (s, 0, c)),
        ],
        out_specs=[
            pl.BlockSpec((SEG, LANES), lambda s, c: (s, c)),
            pl.BlockSpec((1, 2, NCH, LANES), lambda s, c: (s, 0, 0, c)),
        ],
        out_shape=[
            jax.ShapeDtypeStruct((NALL, D_LRU), BF16),
            jax.ShapeDtypeStruct((nseg, 2, NCH, D_LRU), F32),
        ],
        scratch_shapes=[
            pltpu.VMEM((SEG, LANES), F32),
            pltpu.VMEM((2, NCH * CHS, LANES), F32),
            pltpu.VMEM((2, NCH * CHS, LANES), F32),
            pltpu.VMEM((2, NCH, LANES), F32),
        ],
        compiler_params=_cp(("arbitrary", "arbitrary")),
        name="lru",
    )(u_lru, u_gate, w_conv, b_conv.reshape(1, D_LRU), wg_bd, bg, lam, h0)


RT = D // LANES


def _store_row_tiles(ref, start, n, v):
    for c in range(RT):
        ref[pl.ds(start * RT + c, n, stride=RT), :] = v[:, c * LANES:(c + 1) * LANES]


def _load_row_tiles(ref, start, n):
    return jnp.concatenate([ref[pl.ds(start * RT + c, n, stride=RT), :] for c in range(RT)], axis=1)


def _outproj_kernel(xa_ref, xb_ref, yp_ref, yl_ref, mod_ref, g_ref, wo_ref, wr_ref, x1_ref, hn_ref, aff_ref):
    mix = jnp.dot(yp_ref[...], wo_ref[0:D_POOL, :], preferred_element_type=F32)
    mix = mix + jnp.dot(yl_ref[...], wo_ref[D_POOL:, :], preferred_element_type=F32)
    x1 = _stream_tile(xa_ref, xb_ref, TM) + mod_ref[0, 2:3, :] * mix
    x1_ref[...] = x1
    hn = _rms(x1, g_ref[...]) * (1.0 + mod_ref[0, 4:5, :]) + mod_ref[0, 3:4, :]
    _store_row_tiles(hn_ref, 0, TM, hn)
    hn_hi = hn.astype(BF16)
    hn_lo = (hn - hn_hi.astype(F32)).astype(BF16)
    nt = (((1,), (1,)), ((), ()))
    logits = (lax.dot_general(wr_ref[0], hn_hi, nt, preferred_element_type=F32)
              + lax.dot_general(wr_ref[0], hn_lo, nt, preferred_element_type=F32)
              + lax.dot_general(wr_ref[1], hn_hi, nt, preferred_element_type=F32))
    e = jnp.exp(logits - jnp.max(logits, axis=0, keepdims=True))
    aff_ref[...] = e / jnp.sum(e, axis=0, keepdims=True)


def _outproj(xa, xb, y_pool, y_lru, mod_l, g, w_out_bf, w_router_t):
    return pl.pallas_call(
        _outproj_kernel,
        grid=(NALL // TM,),
        in_specs=_stream_specs(TM) + [
            pl.BlockSpec((TM, D_POOL), lambda i: (i, 0)),
            pl.BlockSpec((TM, D_LRU), lambda i: (i, 0)),
            pl.BlockSpec((1, 8, D), lambda i: (_mod_row(i), 0, 0)),
            pl.BlockSpec((1, D), lambda i: (0, 0)),
            pl.BlockSpec((D, D), lambda i: (0, 0)),
            pl.BlockSpec((2, N_EXPERTS, D), lambda i: (0, 0, 0)),
        ],
        out_specs=[
            pl.BlockSpec((TM, D), lambda i: (i, 0)),
            pl.BlockSpec((TM * RT, LANES), lambda i: (i, 0)),
            pl.BlockSpec((N_EXPERTS, TM), lambda i: (0, i)),
        ],
        out_shape=[
            jax.ShapeDtypeStruct((NALL, D), F32),
            jax.ShapeDtypeStruct((NALL * RT, LANES), F32),
            jax.ShapeDtypeStruct((N_EXPERTS, NALL), F32),
        ],
        compiler_params=_cp(("arbitrary",)),
        name="outproj",
    )(xa, xb, y_pool, y_lru, mod_l, g.reshape(1, D), w_out_bf, w_router_t)


def _lane_prefix(m, tri):
    parts = []
    bounds = []
    run = jnp.zeros((m.shape[0], 1), F32)
    for k in range(NTOK // LANES):
        if (k * LANES) % TT == 0:
            bounds.append(run)
        p = jnp.dot(m[:, k * LANES:(k + 1) * LANES].astype(BF16), tri, preferred_element_type=F32)
        parts.append(p + run)
        run = run + p[:, LANES - 1:LANES]
    bounds.append(run)
    return jnp.concatenate(parts, axis=1), bounds


def _plan_kernel(aff_ref, slot_ref, kord_ref, cnt_ref, off_ref, maxk_ref):
    aff = aff_ref[...]
    bits = pltpu.bitcast(aff, jnp.int32)

    def count_ge(v):
        return jnp.sum(jnp.where(bits >= v, 1.0, 0.0), axis=1, keepdims=True)

    def bisect(_, c):
        lo, hi = c
        mid = lo + jnp.right_shift(hi - lo + 1, 1)
        ok = count_ge(mid) >= float(CAP)
        return jnp.where(ok, mid, lo), jnp.where(ok, hi, mid - 1)
    lo0 = jnp.zeros((N_EXPERTS, 1), jnp.int32)
    hi0 = jnp.full((N_EXPERTS, 1), 0x7F7FFFFF, jnp.int32)
    thr, _ = lax.fori_loop(0, 31, bisect, (lo0, hi0))

    ri = lax.broadcasted_iota(jnp.int32, (LANES, LANES), 0)
    ci = lax.broadcasted_iota(jnp.int32, (LANES, LANES), 1)
    tri = jnp.where(ri <= ci, 1.0, 0.0).astype(BF16)

    gt = bits > thr
    eq = bits == thr
    need = float(CAP) - jnp.sum(jnp.where(gt, 1.0, 0.0), axis=1, keepdims=True)
    eq_rank, _ = _lane_prefix(jnp.where(eq, 1.0, 0.0), tri)
    sel = jnp.where(gt, 1.0, jnp.where(eq & (eq_rank <= need), 1.0, 0.0))
    rank, bounds = _lane_prefix(sel, tri)
    slot_ref[0] = jnp.where(sel > 0.0, rank - 1.0, -1.0)

    er = lax.broadcasted_iota(jnp.int32, (N_EXPERTS, N_EXPERTS), 0)
    ec = lax.broadcasted_iota(jnp.int32, (N_EXPERTS, N_EXPERTS), 1)
    below = jnp.where(ec < er, 1.0, 0.0).astype(BF16)
    kord_ref[0] = jnp.dot(below, sel.astype(BF16), preferred_element_type=F32)
    cnt = jnp.sum(sel, axis=0, keepdims=True)
    cnt_ref[0] = cnt

    lane = lax.broadcasted_iota(jnp.int32, (N_EXPERTS, LANES), 1)
    offs = jnp.zeros((N_EXPERTS, LANES), F32)
    for b, col in enumerate(bounds):
        offs = jnp.where(lane == b, col, offs)
    off_ref[0] = offs.astype(jnp.int32)

    lane1 = lax.broadcasted_iota(jnp.int32, (1, LANES), 1)
    mk = jnp.zeros((1, LANES), F32)
    for b in range(NTT):
        mk = jnp.where(lane1 == b, jnp.max(cnt[:, b * TT:(b + 1) * TT], axis=1, keepdims=True), mk)
    maxk_ref[0] = mk.astype(jnp.int32)


def _plan(aff_t):
    ns = NALL // NTOK
    return pl.pallas_call(
        _plan_kernel,
        grid=(ns,),
        in_specs=[pl.BlockSpec((N_EXPERTS, NTOK), lambda s: (0, s))],
        out_specs=[
            pl.BlockSpec((1, N_EXPERTS, NTOK), lambda s: (s, 0, 0)),
            pl.BlockSpec((1, N_EXPERTS, NTOK), lambda s: (s, 0, 0)),
            pl.BlockSpec((1, 1, NTOK), lambda s: (s, 0, 0)),
            pl.BlockSpec((1, N_EXPERTS, LANES), lambda s: (s, 0, 0)),
            pl.BlockSpec((1, 1, LANES), lambda s: (s, 0, 0)),
        ],
        out_shape=[
            jax.ShapeDtypeStruct((ns, N_EXPERTS, NTOK), F32),
            jax.ShapeDtypeStruct((ns, N_EXPERTS, NTOK), F32),
            jax.ShapeDtypeStruct((ns, 1, NTOK), F32),
            jax.ShapeDtypeStruct((ns, N_EXPERTS, LANES), jnp.int32),
            jax.ShapeDtypeStruct((ns, 1, LANES), jnp.int32),
        ],
        compiler_params=_cp(("arbitrary",)),
        name="plan",
    )(aff_t)


COMPACT_UNROLL = 8


def _compact_kernel(off_ref, slot_ref, aff_ref, kord_ref, o_ref):
    p = pl.program_id(0)
    nj = CAP // TT
    sub = lax.broadcasted_iota(jnp.int32, (TT, TT), 0).astype(F32)
    lane = lax.broadcasted_iota(jnp.int32, (1, TT), 1).astype(F32)
    lane8 = lax.broadcasted_iota(jnp.int32, (8, TT), 1)

    def tile(b, accs):
        lo = off_ref[p * LANES + b]
        t0 = pl.multiple_of(b * TT, TT)
        srow = slot_ref[0, :, pl.ds(t0, TT)]
        g = aff_ref[0, :, pl.ds(t0, TT)]
        g_hi = g.astype(BF16)
        r1 = g - g_hi.astype(F32)
        g_mid = r1.astype(BF16)
        g_lo = (r1 - g_mid.astype(F32)).astype(BF16)
        tile_id = jnp.full((1, TT), b, jnp.int32).astype(F32)
        data = jnp.concatenate([
            tile_id.astype(BF16), lane.astype(BF16), g_hi, g_mid, g_lo,
            kord_ref[0, :, pl.ds(t0, TT)].astype(BF16),
            jnp.zeros((2, TT), BF16)], axis=0)
        onehot = jnp.where(srow - lo.astype(F32) == sub, 1.0, 0.0).astype(BF16)
        res = lax.dot_general(data, onehot, (((1,), (1,)), ((), ())), preferred_element_type=F32)
        shift = lo & (TT - 1)
        j0 = lo // TT
        rolled = pltpu.roll(res, shift, axis=1)
        first = lane8 >= shift
        out = []
        for j in range(nj):
            mine = jnp.logical_or(jnp.logical_and(first, j == j0),
                                  jnp.logical_and(jnp.logical_not(first), j == j0 + 1))
            out.append(accs[j] + jnp.where(mine, rolled, 0.0))
        return tuple(out)

    def tiles(i, accs):
        for u in range(COMPACT_UNROLL):
            accs = tile(i * COMPACT_UNROLL + u, accs)
        return accs
    zero = jnp.zeros((8, TT), F32)
    accs = lax.fori_loop(0, NTT // COMPACT_UNROLL, tiles, (zero,) * nj)
    for j in range(nj):
        o_ref[0, :, j * TT:(j + 1) * TT] = accs[j]


def _compact(offs, slot, aff_rows, kord):
    npair = (NALL // NTOK) * N_EXPERTS
    spec = pl.BlockSpec((1, 1, NTOK), lambda p, off: (p, 0, 0))
    return pl.pallas_call(
        _compact_kernel,
        grid_spec=pltpu.PrefetchScalarGridSpec(
            num_scalar_prefetch=1,
            grid=(npair,),
            in_specs=[spec, spec, spec],
            out_specs=pl.BlockSpec((1, 8, CAP), lambda p, off: (p, 0, 0)),
        ),
        out_shape=jax.ShapeDtypeStruct((npair, 8, CAP), F32),
        compiler_params=_cp(("arbitrary",)),
        name="compact",
    )(offs.reshape(-1), slot.reshape(npair, 1, NTOK), aff_rows.reshape(npair, 1, NTOK),
      kord.reshape(npair, 1, NTOK))


FSPLIT = 2
NPAIR = (NALL // NTOK) * N_EXPERTS
GATHER_UNROLL = 8


def _ffn_kernel(idx_ref, hn_ref, wg_ref, wu_ref, wd_ref, g_ref, o_ref, xs_a, xs_b, acc_ref, sem):
    p = pl.program_id(0)
    f = pl.program_id(1)
    share = CAP // FSPLIT

    def start_row(pair, dst_ref, dst_sem, s):
        src = pl.multiple_of(idx_ref[pair * CAP + s] * RT, RT)
        pltpu.make_async_copy(hn_ref.at[pl.ds(src, RT)], dst_ref.at[pl.ds(s * RT, RT)], dst_sem).start()

    def wait_rows(ref, ref_sem):
        pltpu.make_async_copy(hn_ref.at[pl.ds(0, CAP * RT)], ref, ref_sem).wait()

    @pl.when(jnp.logical_and(p == 0, f == 0))
    def _():
        acc_ref[...] = jnp.zeros_like(acc_ref)

        def issue(i, c):
            for u in range(GATHER_UNROLL):
                start_row(0, xs_a, sem.at[0], i * GATHER_UNROLL + u)
            return c
        lax.fori_loop(0, CAP // GATHER_UNROLL, issue, 0)

    def step(cur, cur_sem, nxt, nxt_sem):
        @pl.when(f == 0)
        def _():
            wait_rows(cur, cur_sem)

        nxt_pair = jnp.minimum(p + 1, NPAIR - 1)
        for s in range(share):
            start_row(nxt_pair, nxt, nxt_sem, f * share + s)

        x = _load_row_tiles(cur, 0, CAP).astype(BF16)
        hg = jnp.dot(x, wg_ref[0, 0].astype(BF16), preferred_element_type=F32)
        hu = jnp.dot(x, wu_ref[0, 0].astype(BF16), preferred_element_type=F32)
        hid = (hg * _sigmoid(hg) * hu).astype(BF16)
        o = jnp.dot(hid, wd_ref[0, 0].astype(BF16), preferred_element_type=F32)

        acc = jnp.where(f > 0, acc_ref[...], 0.0) + o
        acc_ref[...] = acc
        _store_row_tiles(o_ref, 0, CAP, acc * g_ref[0][:, 0:1])

        @pl.when(jnp.logical_and(p == NPAIR - 1, f == FSPLIT - 1))
        def _():
            wait_rows(nxt, nxt_sem)

    @pl.when(p % 2 == 0)
    def _():
        step(xs_a, sem.at[0], xs_b, sem.at[1])

    @pl.when(p % 2 == 1)
    def _():
        step(xs_b, sem.at[1], xs_a, sem.at[0])


def _ffn(l, idx_rows, hn, w_gate, w_up, w_down, g_cols):
    fw = D_EXPERT // FSPLIT
    return pl.pallas_call(
        _ffn_kernel,
        grid_spec=pltpu.PrefetchScalarGridSpec(
            num_scalar_prefetch=1,
            grid=(NPAIR, FSPLIT),
            in_specs=[
                pl.BlockSpec(memory_space=pl.ANY),
                pl.BlockSpec((1, 1, D, fw), lambda p, f, idx: (l, p % N_EXPERTS, 0, f)),
                pl.BlockSpec((1, 1, D, fw), lambda p, f, idx: (l, p % N_EXPERTS, 0, f)),
                pl.BlockSpec((1, 1, fw, D), lambda p, f, idx: (l, p % N_EXPERTS, f, 0)),
                pl.BlockSpec((1, CAP, LANES), lambda p, f, idx: (p, 0, 0)),
            ],
            out_specs=pl.BlockSpec((CAP * RT, LANES), lambda p, f, idx: (p, 0)),
            scratch_shapes=[
                pltpu.VMEM((CAP * RT, LANES), F32),
                pltpu.VMEM((CAP * RT, LANES), F32),
                pltpu.VMEM((CAP, D), F32),
                pltpu.SemaphoreType.DMA((2,)),
            ],
        ),
        out_shape=jax.ShapeDtypeStruct((NPAIR * CAP * RT, LANES), F32),
        compiler_params=_cp(("arbitrary", "arbitrary")),
        name="ffn",
    )(idx_rows, hn, w_gate, w_up, w_down, g_cols)


COMBINE_UNROLL = 4
STAGE_ROWS = N_EXPERTS * TT


def _combine_kernel(off_ref, dst_ref, maxk_ref, x1_ref, mod_ref, cnt_ref, gf_ref, eo_ref,
                    octx_ref, o_ref, stage_ref, sem, *, final):
    i = pl.program_id(0)
    ntile = pl.num_programs(0)
    buf = i % 2

    def fetch_tile(tile, slot):
        s = tile // NTT
        b = tile % NTT
        slot_base = slot * STAGE_ROWS

        def fetch(r):
            dst = pl.multiple_of((slot_base + dst_ref[r]) * RT, RT)
            pltpu.make_async_copy(eo_ref.at[pl.ds(pl.multiple_of(r * RT, RT), RT)], stage_ref.at[pl.ds(dst, RT)],
                                  sem.at[slot]).start()

        for e in range(N_EXPERTS):
            pe = s * N_EXPERTS + e
            lo = off_ref[pe * LANES + b]
            n = off_ref[pe * LANES + b + 1] - lo
            base = pe * CAP + lo
            nq = n // COMBINE_UNROLL

            def issue_many(q, c, base=base):
                for u in range(COMBINE_UNROLL):
                    fetch(base + q * COMBINE_UNROLL + u)
                return c
            lax.fori_loop(0, nq, issue_many, 0)

            def issue_one(r, c):
                fetch(r)
                return c
            lax.fori_loop(base + nq * COMBINE_UNROLL, base + n, issue_one, 0)

    def rows_of_tile(tile):
        s = tile // NTT
        b = tile % NTT
        total = 0
        for e in range(N_EXPERTS):
            pe = s * N_EXPERTS + e
            total = total + off_ref[pe * LANES + b + 1] - off_ref[pe * LANES + b]
        return total

    @pl.when(i == 0)
    def _():
        fetch_tile(i, buf)

    @pl.when(i + 1 < ntile)
    def _():
        fetch_tile(i + 1, 1 - buf)

    total = rows_of_tile(i)

    @pl.when(total > 0)
    def _():
        pltpu.make_async_copy(eo_ref.at[pl.ds(0, total * RT)], stage_ref.at[pl.ds(0, total * RT)],
                              sem.at[buf]).wait()

    cnt = cnt_ref[:, 0:1]
    o_ref[...] = jnp.zeros_like(o_ref)

    def add(k, c):
        rows = _load_row_tiles(stage_ref, pl.multiple_of(buf * STAGE_ROWS + k * TT, TT), TT)
        o_ref[...] += jnp.where(cnt > k.astype(F32), rows, 0.0)
        return c
    lax.fori_loop(0, maxk_ref[(i // NTT) * LANES + i % NTT], add, 0)
    x2 = x1_ref[...] + mod_ref[0, 5:6, :] * o_ref[...]
    if final:
        x2 = _rms(x2, gf_ref[...])
    o_ref[...] = x2

    @pl.when(i < NTT)
    def _():
        octx_ref[...] = x2


def _combine(offs, dst_rows, maxk, x1, mod_l, cnt_cols, g_final, eo, final):
    def mrow(i):
        return jnp.where(i < NTT, 0, 1 + (i - NTT) // (LAT_L // TT))
    return pl.pallas_call(
        functools.partial(_combine_kernel, final=final),
        grid_spec=pltpu.PrefetchScalarGridSpec(
            num_scalar_prefetch=3,
            grid=(NALL // TT,),
            in_specs=[
                pl.BlockSpec((TT, D), lambda i, *_: (i, 0)),
                pl.BlockSpec((1, 8, D), lambda i, *_: (mrow(i), 0, 0)),
                pl.BlockSpec((TT, LANES), lambda i, *_: (i, 0)),
                pl.BlockSpec((1, D), lambda i, *_: (0, 0)),
                pl.BlockSpec(memory_space=pl.ANY),
            ],
            out_specs=_stream_specs(TT),
            scratch_shapes=[pltpu.VMEM((2 * STAGE_ROWS * RT, LANES), F32), pltpu.SemaphoreType.DMA((2,))],
        ),
        out_shape=[jax.ShapeDtypeStruct((NTOK, D), F32)] * 2,
        compiler_params=_cp(("arbitrary",)),
        name="combine",
    )(offs.reshape(-1), dst_rows, maxk.reshape(-1), x1, mod_l, cnt_cols, g_final.reshape(1, D), eo)


def _gate_blocks(w_a, w_x):
    hd = w_a.shape[-1]
    per = LANES // hd
    ncb = D_LRU // LANES
    mats = jnp.stack([w_a[0], w_x[0], w_a[1], w_x[1]], axis=0)
    mats = mats.reshape(4, ncb, per, hd, hd)
    out = jnp.zeros((ncb, 4, LANES, LANES), F32)
    for q in range(per):
        out = out.at[:, :, q * hd:(q + 1) * hd, q * hd:(q + 1) * hd].set(mats[:, :, q].transpose(1, 0, 2, 3))
    return out.astype(BF16)


def kernel(x_prompt, x_sample, state_rglru, c, c_ctx, w_mod, b_mod, g_norm_mix, g_norm_ffn, w_in, w_pool,
           pool_scale, w_conv, b_conv, w_gate_a, b_gate_a, w_gate_x, b_gate_x, lru_lambda, w_out, w_router,
           w_exp_gate, w_exp_up, w_exp_down, g_final):
    nb_ctx = x_prompt.shape[0]
    nb_lat = x_sample.shape[0]
    xa, xb = x_prompt.reshape(NTOK, D), x_sample.reshape(NTOK, D)

    cvec = jnp.zeros((8, D), F32).at[0].set(c_ctx).at[1:1 + nb_lat].set(c)
    mod = _modulation(cvec, w_mod, b_mod)
    mod = jnp.pad(mod.transpose(0, 2, 1, 3), ((0, 0), (0, 0), (0, 2), (0, 0)))

    nseg = NALL // SEG
    states = []
    for l in range(DEPTH):
        u_pool, u_lru, u_gate = _inproj(xa, xb, mod[l], g_norm_mix[l], w_in[l].astype(BF16))
        y_pool = _pool(u_pool, w_pool[l].astype(BF16), pool_scale[l])
        h0 = jnp.zeros((nseg, 2, D_LRU), F32).at[NTOK // SEG:].set(state_rglru[:, l])
        bg = jnp.stack([b_gate_a[l, 0], b_gate_x[l, 0], b_gate_a[l, 1], b_gate_x[l, 1]], axis=0).reshape(4, D_LRU)
        y_lru, st = _lru(u_lru, u_gate, w_conv[l], b_conv[l], _gate_blocks(w_gate_a[l], w_gate_x[l]), bg,
                         lru_lambda[l], h0)
        states.append(st[:NTOK // SEG].transpose(0, 2, 1, 3).reshape(nb_ctx, 2, D_LRU))

        wr = w_router[l].T
        wr_hi = wr.astype(BF16)
        wr_split = jnp.stack([wr_hi, (wr - wr_hi.astype(F32)).astype(BF16)], axis=0)
        x1, hn, aff_t = _outproj(xa, xb, y_pool, y_lru, mod[l], g_norm_ffn[l], w_out[l].astype(BF16), wr_split)
        slot, kord, cnt, offs, maxk = _plan(aff_t)
        aff_rows = aff_t.reshape(N_EXPERTS, NALL // NTOK, NTOK).transpose(1, 0, 2)
        lists = _compact(offs, slot, aff_rows, kord)
        tok_local = (lists[:, 0] * TT + lists[:, 1]).astype(jnp.int32)
        g_sel = lists[:, 2] + lists[:, 3] + lists[:, 4]
        kord_i = lists[:, 5].astype(jnp.int32)
        stream_base = (jnp.arange(NPAIR, dtype=jnp.int32) // N_EXPERTS) * NTOK
        src_rows = (tok_local + stream_base[:, None]).reshape(-1)
        g_cols = jnp.broadcast_to(g_sel[:, :, None], g_sel.shape + (LANES,))
        eo = _ffn(l, src_rows, hn, w_exp_gate, w_exp_up, w_exp_down, g_cols)
        cnt_cols = jnp.broadcast_to(cnt.reshape(NALL, 1), (NALL, LANES))
        dst_rows = (kord_i * TT + tok_local % TT).reshape(-1)
        xa, xb = _combine(offs, dst_rows, maxk, x1, mod[l], cnt_cols, g_final, eo, final=(l == DEPTH - 1))

    return xa.reshape(x_prompt.shape), xb.reshape(x_sample.shape), jnp.stack(states, axis=1)
```

```python
import functools

import jax
import jax.numpy as jnp
from jax import lax
from jax.experimental import pallas as pl
from jax.experimental.pallas import tpu as pltpu

F32 = jnp.float32
BF16 = jnp.bfloat16
MM = jnp.float32
HIGHEST = lax.Precision.HIGHEST

D = 1024
NTOK = 8192
NALL = 2 * NTOK
CTX_L = 256
LAT_L = 4096
GRID_W = 64
DEPTH = 2
D_POOL = 512
D_LRU = 512
D_IN = D_POOL + 2 * D_LRU
POOL_WINDOWS = (2, 4, 8, 16)
LANES = 128
N_EXPERTS = 16
CAP = 2 * NTOK // N_EXPERTS
D_EXPERT = 1024
LRU_C = 8.0
EPS = 1e-6

TM = 512
SEG = 4096
CH = 256
NCH = SEG // CH
CHS = CH + 8
TT = 256
NTT = NTOK // TT
VMEM_LIMIT = 56 * 1024 * 1024


def _cp(sem, vmem=VMEM_LIMIT):
    return pltpu.CompilerParams(dimension_semantics=sem, vmem_limit_bytes=vmem)


def _sigmoid(x):
    return 0.5 * jnp.tanh(0.5 * x) + 0.5


def _mod_row(i):
    n_ctx = NTOK // TM
    return jnp.where(i < n_ctx, 0, 1 + (i - n_ctx) // (LAT_L // TM))


def _mod_kernel(c_ref, w_ref, b_ref, o_ref):
    cv = c_ref[...]
    s = cv * jax.nn.sigmoid(cv)
    o_ref[0, 0] = jnp.dot(s, w_ref[0], precision=HIGHEST, preferred_element_type=F32) + b_ref[0, 0]


def _modulation(cvec, w_mod, b_mod):
    return pl.pallas_call(
        _mod_kernel,
        grid=(DEPTH, 6),
        in_specs=[
            pl.BlockSpec((8, D), lambda l, j: (0, 0)),
            pl.BlockSpec((1, D, D), lambda l, j: (l, 0, j)),
            pl.BlockSpec((1, 1, 1, D), lambda l, j: (l, j, 0, 0)),
        ],
        out_specs=pl.BlockSpec((1, 1, 8, D), lambda l, j: (l, j, 0, 0)),
        out_shape=jax.ShapeDtypeStruct((DEPTH, 6, 8, D), F32),
        compiler_params=_cp(("arbitrary", "arbitrary")),
        name="modulation",
    )(cvec, w_mod, b_mod.reshape(DEPTH, 6, 1, D))


def _rms(x, g):
    return x * lax.rsqrt(jnp.mean(x * x, axis=-1, keepdims=True) + EPS) * g


def _stream_specs(tile):
    n = NTOK // tile
    return [pl.BlockSpec((tile, D), lambda i, *_: (jnp.minimum(i, n - 1), 0)),
            pl.BlockSpec((tile, D), lambda i, *_: (jnp.maximum(i - n, 0), 0))]


def _stream_tile(xa_ref, xb_ref, tile):
    return jnp.where(pl.program_id(0) < NTOK // tile, xa_ref[...], xb_ref[...])


def _inproj_kernel(xa_ref, xb_ref, mod_ref, g_ref, w_ref, up_ref, ul_ref, ug_ref):
    x = _stream_tile(xa_ref, xb_ref, TM)
    hn = _rms(x, g_ref[...]) * (1.0 + mod_ref[0, 1:2, :]) + mod_ref[0, 0:1, :]
    proj = jnp.dot(hn.astype(MM), w_ref[...], preferred_element_type=F32)
    up_ref[...] = proj[:, :D_POOL]
    ul_ref[...] = proj[:, D_POOL:D_POOL + D_LRU]
    ug_ref[...] = proj[:, D_POOL + D_LRU:]


def _inproj(xa, xb, mod_l, g, w_in_bf):
    out = jax.ShapeDtypeStruct((NALL, D_POOL), F32)
    return pl.pallas_call(
        _inproj_kernel,
        grid=(NALL // TM,),
        in_specs=_stream_specs(TM) + [
            pl.BlockSpec((1, 8, D), lambda i: (_mod_row(i), 0, 0)),
            pl.BlockSpec((1, D), lambda i: (0, 0)),
            pl.BlockSpec((D, D_IN), lambda i: (0, 0)),
        ],
        out_specs=[pl.BlockSpec((TM, D_POOL), lambda i: (i, 0))] * 3,
        out_shape=[out, out, out],
        compiler_params=_cp(("arbitrary",)),
        name="inproj",
    )(xa, xb, mod_l, g.reshape(1, D), w_in_bf)


def _window_sum(x, h, pos, period):
    n = x.shape[0]
    acc = x
    for i in range(1, h):
        acc = acc + jnp.where(pos < period - i, pltpu.roll(x, n - i, axis=0), 0.0)
    for i in range(1, h + 1):
        acc = acc + jnp.where(pos >= i, pltpu.roll(x, i, axis=0), 0.0)
    return acc


def _window_count(pos, h, period):
    return (jnp.minimum(pos + h, period) - jnp.maximum(pos - h, 0)).astype(F32)


def _pool_kernel(u_ref, w_ref, s_ref, o_ref, pad_ref):
    seg = pl.program_id(0)
    row = lax.broadcasted_iota(jnp.int32, (CH, LANES), 0)
    halo = 8 * GRID_W

    def project(d, g, t0):
        y = jnp.dot(d.astype(MM), w_ref[g], preferred_element_type=F32)
        y = y * s_ref[:, g * LANES:(g + 1) * LANES]
        o_ref[pl.ds(t0, CH), g * LANES:(g + 1) * LANES] = y.astype(MM)

    @pl.when(seg < NTOK // SEG)
    def _context():
        def body(sb, c):
            t0 = pl.multiple_of(sb * CH, CH)
            for g, w in enumerate(POOL_WINDOWS):
                h = w // 2
                x = u_ref[pl.ds(t0, CH), g * LANES:(g + 1) * LANES]
                m = _window_sum(x, h, row, CTX_L) / _window_count(row, h, CTX_L)
                project(m - x, g, t0)
            return c
        lax.fori_loop(0, NCH, body, 0)

    @pl.when(seg >= NTOK // SEG)
    def _latent():
        zeros = jnp.zeros((halo, D_POOL), F32)
        pad_ref[0:halo, :] = zeros
        pad_ref[halo + SEG:halo + SEG + halo, :] = zeros
        col = row & (GRID_W - 1)

        def cols(sb, c):
            t0 = pl.multiple_of(sb * CH, CH)
            for g, w in enumerate(POOL_WINDOWS):
                h = w // 2
                x = u_ref[pl.ds(t0, CH), g * LANES:(g + 1) * LANES]
                m = _window_sum(x, h, col, GRID_W) / _window_count(col, h, GRID_W)
                pad_ref[pl.ds(halo + t0, CH), g * LANES:(g + 1) * LANES] = m
            return c
        lax.fori_loop(0, NCH, cols, 0)

        def rows(sb, c):
            t0 = pl.multiple_of(sb * CH, CH)
            r = jnp.right_shift(t0 + row, 6)
            for g, w in enumerate(POOL_WINDOWS):
                h = w // 2
                acc = None
                for i in range(-h, h):
                    start = pl.multiple_of(halo + t0 + i * GRID_W, GRID_W)
                    v = pad_ref[pl.ds(start, CH), g * LANES:(g + 1) * LANES]
                    acc = v if acc is None else acc + v
                m = acc / _window_count(r, h, GRID_W)
                x = u_ref[pl.ds(t0, CH), g * LANES:(g + 1) * LANES]
                project(m - x, g, t0)
            return c
        lax.fori_loop(0, NCH, rows, 0)


def _pool(u_pool, w_pool_bf, pool_scale):
    halo = 8 * GRID_W
    return pl.pallas_call(
        _pool_kernel,
        grid=(NALL // SEG,),
        in_specs=[
            pl.BlockSpec((SEG, D_POOL), lambda i: (i, 0)),
            pl.BlockSpec((len(POOL_WINDOWS), LANES, LANES), lambda i: (0, 0, 0)),
            pl.BlockSpec((1, D_POOL), lambda i: (0, 0)),
        ],
        out_specs=pl.BlockSpec((SEG, D_POOL), lambda i: (i, 0)),
        out_shape=jax.ShapeDtypeStruct((NALL, D_POOL), MM),
        scratch_shapes=[pltpu.VMEM((SEG + 2 * halo, D_POOL), F32)],
        compiler_params=_cp(("arbitrary",)),
        name="pool",
    )(u_pool, w_pool_bf, pool_scale.reshape(1, D_POOL))


def _lru_kernel(ul_ref, ug_ref, wc_ref, bc_ref, wg_ref, bg_ref, lam_ref, h0_ref,
                o_ref, st_ref, xc_ref, a_ref, b_ref, hin_ref):
    seg = pl.program_id(0)
    is_lat = seg >= NTOK // SEG
    last = jnp.where(is_lat, LAT_L - 1, CTX_L - 1)

    x = ul_ref[...]
    pos = lax.broadcasted_iota(jnp.int32, (SEG, LANES), 0) & last
    xc = bc_ref[...] + wc_ref[0:1, :] * jnp.where(pos >= 2, pltpu.roll(x, 2, axis=0), 0.0)
    xc = xc + wc_ref[1:2, :] * jnp.where(pos >= 1, pltpu.roll(x, 1, axis=0), 0.0)
    xc = xc + wc_ref[2:3, :] * x
    xc = xc + wc_ref[3:4, :] * jnp.where(pos < last, pltpu.roll(x, SEG - 1, axis=0), 0.0)
    xc_ref[...] = xc

    def gates(c, carry):
        t0 = pl.multiple_of(c * CH, CH)
        xcb = xc_ref[pl.ds(t0, CH), :]
        xb = xcb.astype(MM)
        for d in range(2):
            r = _sigmoid(jnp.dot(xb, wg_ref[0, 2 * d], preferred_element_type=F32) + bg_ref[2 * d:2 * d + 1, :])
            i = _sigmoid(jnp.dot(xb, wg_ref[0, 2 * d + 1], preferred_element_type=F32) + bg_ref[2 * d + 1:2 * d + 2, :])
            log_a = (-LRU_C) * r * jax.nn.softplus(-lam_ref[d:d + 1, :])
            a = jnp.exp(log_a)
            m2 = jnp.maximum(-jnp.tanh(log_a) * (a * a + 1.0), 0.0)
            mult = jnp.where(m2 > 0.0, m2 * lax.rsqrt(m2), 0.0)
            s0 = pl.multiple_of(c * CHS, 8)
            a_ref[d, pl.ds(s0, CH), :] = a
            b_ref[d, pl.ds(s0, CH), :] = mult * (i * xcb)
        return carry
    lax.fori_loop(0, NCH, gates, 0)

    def scan(j, carry):
        hf, pf, hb, pb = carry
        jb = CH - 1 - j
        sf = pl.ds(j, NCH, stride=CHS)
        sb = pl.ds(jb, NCH, stride=CHS)
        af = a_ref[0, sf, :]
        bf = b_ref[0, sf, :]
        ab = a_ref[1, sb, :]
        bb = b_ref[1, sb, :]
        hf = af * hf + bf
        pf = af * pf
        hb = ab * hb + bb
        pb = ab * pb
        b_ref[0, sf, :] = hf
        a_ref[0, sf, :] = pf
        b_ref[1, sb, :] = hb
        a_ref[1, sb, :] = pb
        return hf, pf, hb, pb
    z = jnp.zeros((NCH, LANES), F32)
    o = jnp.ones((NCH, LANES), F32)
    hf_end, pf_end, hb_beg, pb_beg = lax.fori_loop(0, CH, scan, (z, o, z, o))

    cur = h0_ref[0, 0:1, :]
    rows_f = []
    for c in range(NCH):
        rows_f.append(cur)
        cur = pf_end[c:c + 1, :] * cur + hf_end[c:c + 1, :]
    cur = h0_ref[0, 1:2, :]
    rows_b = [None] * NCH
    for c in range(NCH - 1, -1, -1):
        rows_b[c] = cur
        cur = pb_beg[c:c + 1, :] * cur + hb_beg[c:c + 1, :]
    hin_f = jnp.where(is_lat, jnp.concatenate(rows_f, axis=0), 0.0)
    hin_b = jnp.where(is_lat, jnp.concatenate(rows_b, axis=0), 0.0)
    hin_ref[0] = hin_f
    hin_ref[1] = hin_b
    st_ref[0, 0] = pf_end * hin_f + hf_end
    st_ref[0, 1] = pb_beg * hin_b + hb_beg

    def finish(c, carry):
        t0 = pl.multiple_of(c * CH, CH)
        rows = pl.ds(pl.multiple_of(c * CHS, 8), CH)
        hf = a_ref[0, rows, :] * hin_ref[0, pl.ds(c, 1), :] + b_ref[0, rows, :]
        hb = a_ref[1, rows, :] * hin_ref[1, pl.ds(c, 1), :] + b_ref[1, rows, :]
        y = jax.nn.gelu(ug_ref[pl.ds(t0, CH), :]) * (hf + hb)
        o_ref[pl.ds(t0, CH), :] = y.astype(MM)
        return carry
    lax.fori_loop(0, NCH, finish, 0)


def _lru(u_lru, u_gate, w_conv, b_conv, wg_bd, bg, lam, h0):
    nseg = NALL // SEG
    ncb = D_LRU // LANES
    return pl.pallas_call(
        _lru_kernel,
        grid=(nseg, ncb),
        in_specs=[
            pl.BlockSpec((SEG, LANES), lambda s, c: (s, c)),
            pl.BlockSpec((SEG, LANES), lambda s, c: (s, c)),
            pl.BlockSpec((4, LANES), lambda s, c: (0, c)),
            pl.BlockSpec((1, LANES), lambda s, c: (0, c)),
            pl.BlockSpec((1, 4, LANES, LANES), lambda s, c: (c, 0, 0, 0)),
            pl.BlockSpec((4, LANES), lambda s, c: (0, c)),
            pl.BlockSpec((2, LANES), lambda s, c: (0, c)),
            pl.BlockSpec((1, 2, LANES), lambda s, c: (s, 0, c)),
        ],
        out_specs=[
            pl.BlockSpec((SEG, LANES), lambda s, c: (s, c)),
            pl.BlockSpec((1, 2, NCH, LANES), lambda s, c: (s, 0, 0, c)),
        ],
        out_shape=[
            jax.ShapeDtypeStruct((NALL, D_LRU), MM),
            jax.ShapeDtypeStruct((nseg, 2, NCH, D_LRU), F32),
        ],
        scratch_shapes=[
            pltpu.VMEM((SEG, LANES), F32),
            pltpu.VMEM((2, NCH * CHS, LANES), F32),
            pltpu.VMEM((2, NCH * CHS, LANES), F32),
            pltpu.VMEM((2, NCH, LANES), F32),
        ],
        compiler_params=_cp(("arbitrary", "arbitrary")),
        name="lru",
    )(u_lru, u_gate, w_conv, b_conv.reshape(1, D_LRU), wg_bd, bg, lam, h0)


RT = D // LANES


def _store_row_tiles(ref, start, n, v):
    for c in range(RT):
        ref[pl.ds(start * RT + c, n, stride=RT), :] = v[:, c * LANES:(c + 1) * LANES]


def _load_row_tiles(ref, start, n):
    return jnp.concatenate([ref[pl.ds(start * RT + c, n, stride=RT), :] for c in range(RT)], axis=1)


def _outproj_kernel(xa_ref, xb_ref, yp_ref, yl_ref, mod_ref, g_ref, wo_ref, wr_ref, x1_ref, hn_ref, aff_ref):
    mix = jnp.dot(yp_ref[...], wo_ref[0:D_POOL, :], preferred_element_type=F32)
    mix = mix + jnp.dot(yl_ref[...], wo_ref[D_POOL:, :], preferred_element_type=F32)
    x1 = _stream_tile(xa_ref, xb_ref, TM) + mod_ref[0, 2:3, :] * mix
    x1_ref[...] = x1
    hn = _rms(x1, g_ref[...]) * (1.0 + mod_ref[0, 4:5, :]) + mod_ref[0, 3:4, :]
    _store_row_tiles(hn_ref, 0, TM, hn)
    hn_hi = hn.astype(BF16)
    hn_lo = (hn - hn_hi.astype(F32)).astype(BF16)
    nt = (((1,), (1,)), ((), ()))
    logits = (lax.dot_general(wr_ref[0], hn_hi, nt, preferred_element_type=F32)
              + lax.dot_general(wr_ref[0], hn_lo, nt, preferred_element_type=F32)
              + lax.dot_general(wr_ref[1], hn_hi, nt, preferred_element_type=F32))
    e = jnp.exp(logits - jnp.max(logits, axis=0, keepdims=True))
    aff_ref[...] = e / jnp.sum(e, axis=0, keepdims=True)


def _outproj(xa, xb, y_pool, y_lru, mod_l, g, w_out_bf, w_router_t):
    return pl.pallas_call(
        _outproj_kernel,
        grid=(NALL // TM,),
        in_specs=_stream_specs(TM) + [
            pl.BlockSpec((TM, D_POOL), lambda i: (i, 0)),
            pl.BlockSpec((TM, D_LRU), lambda i: (i, 0)),
            pl.BlockSpec((1, 8, D), lambda i: (_mod_row(i), 0, 0)),
            pl.BlockSpec((1, D), lambda i: (0, 0)),
            pl.BlockSpec((D, D), lambda i: (0, 0)),
            pl.BlockSpec((2, N_EXPERTS, D), lambda i: (0, 0, 0)),
        ],
        out_specs=[
            pl.BlockSpec((TM, D), lambda i: (i, 0)),
            pl.BlockSpec((TM * RT, LANES), lambda i: (i, 0)),
            pl.BlockSpec((N_EXPERTS, TM), lambda i: (0, i)),
        ],
        out_shape=[
            jax.ShapeDtypeStruct((NALL, D), F32),
            jax.ShapeDtypeStruct((NALL * RT, LANES), F32),
            jax.ShapeDtypeStruct((N_EXPERTS, NALL), F32),
        ],
        compiler_params=_cp(("arbitrary",)),
        name="outproj",
    )(xa, xb, y_pool, y_lru, mod_l, g.reshape(1, D), w_out_bf, w_router_t)


def _lane_prefix(m, tri):
    parts = []
    bounds = []
    run = jnp.zeros((m.shape[0], 1), F32)
    for k in range(NTOK // LANES):
        if (k * LANES) % TT == 0:
            bounds.append(run)
        p = jnp.dot(m[:, k * LANES:(k + 1) * LANES].astype(BF16), tri, preferred_element_type=F32)
        parts.append(p + run)
        run = run + p[:, LANES - 1:LANES]
    bounds.append(run)
    return jnp.concatenate(parts, axis=1), bounds


def _plan_kernel(aff_ref, slot_ref, kord_ref, cnt_ref, off_ref, maxk_ref):
    aff = aff_ref[...]
    bits = pltpu.bitcast(aff, jnp.int32)

    def count_ge(v):
        return jnp.sum(jnp.where(bits >= v, 1.0, 0.0), axis=1, keepdims=True)

    def bisect(_, c):
        lo, hi = c
        mid = lo + jnp.right_shift(hi - lo + 1, 1)
        ok = count_ge(mid) >= float(CAP)
        return jnp.where(ok, mid, lo), jnp.where(ok, hi, mid - 1)
    lo0 = jnp.zeros((N_EXPERTS, 1), jnp.int32)
    hi0 = jnp.full((N_EXPERTS, 1), 0x7F7FFFFF, jnp.int32)
    thr, _ = lax.fori_loop(0, 31, bisect, (lo0, hi0))

    ri = lax.broadcasted_iota(jnp.int32, (LANES, LANES), 0)
    ci = lax.broadcasted_iota(jnp.int32, (LANES, LANES), 1)
    tri = jnp.where(ri <= ci, 1.0, 0.0).astype(BF16)

    gt = bits > thr
    eq = bits == thr
    need = float(CAP) - jnp.sum(jnp.where(gt, 1.0, 0.0), axis=1, keepdims=True)
    eq_rank, _ = _lane_prefix(jnp.where(eq, 1.0, 0.0), tri)
    sel = jnp.where(gt, 1.0, jnp.where(eq & (eq_rank <= need), 1.0, 0.0))
    rank, bounds = _lane_prefix(sel, tri)
    slot_ref[0] = jnp.where(sel > 0.0, rank - 1.0, -1.0)

    er = lax.broadcasted_iota(jnp.int32, (N_EXPERTS, N_EXPERTS), 0)
    ec = lax.broadcasted_iota(jnp.int32, (N_EXPERTS, N_EXPERTS), 1)
    below = jnp.where(ec < er, 1.0, 0.0).astype(BF16)
    kord_ref[0] = jnp.dot(below, sel.astype(BF16), preferred_element_type=F32)
    cnt = jnp.sum(sel, axis=0, keepdims=True)
    cnt_ref[0] = cnt

    lane = lax.broadcasted_iota(jnp.int32, (N_EXPERTS, LANES), 1)
    offs = jnp.zeros((N_EXPERTS, LANES), F32)
    for b, col in enumerate(bounds):
        offs = jnp.where(lane == b, col, offs)
    off_ref[0] = offs.astype(jnp.int32)

    lane1 = lax.broadcasted_iota(jnp.int32, (1, LANES), 1)
    mk = jnp.zeros((1, LANES), F32)
    for b in range(NTT):
        mk = jnp.where(lane1 == b, jnp.max(cnt[:, b * TT:(b + 1) * TT], axis=1, keepdims=True), mk)
    maxk_ref[0] = mk.astype(jnp.int32)


def _plan(aff_t):
    ns = NALL // NTOK
    return pl.pallas_call(
        _plan_kernel,
        grid=(ns,),
        in_specs=[pl.BlockSpec((N_EXPERTS, NTOK), lambda s: (0, s))],
        out_specs=[
            pl.BlockSpec((1, N_EXPERTS, NTOK), lambda s: (s, 0, 0)),
            pl.BlockSpec((1, N_EXPERTS, NTOK), lambda s: (s, 0, 0)),
            pl.BlockSpec((1, 1, NTOK), lambda s: (s, 0, 0)),
            pl.BlockSpec((1, N_EXPERTS, LANES), lambda s: (s, 0, 0)),
            pl.BlockSpec((1, 1, LANES), lambda s: (s, 0, 0)),
        ],
        out_shape=[
            jax.ShapeDtypeStruct((ns, N_EXPERTS, NTOK), F32),
            jax.ShapeDtypeStruct((ns, N_EXPERTS, NTOK), F32),
            jax.ShapeDtypeStruct((ns, 1, NTOK), F32),
            jax.ShapeDtypeStruct((ns, N_EXPERTS, LANES), jnp.int32),
            jax.ShapeDtypeStruct((ns, 1, LANES), jnp.int32),
        ],
        compiler_params=_cp(("arbitrary",)),
        name="plan",
    )(aff_t)


COMPACT_UNROLL = 8


def _compact_kernel(off_ref, slot_ref, aff_ref, kord_ref, o_ref):
    p = pl.program_id(0)
    nj = CAP // TT
    sub = lax.broadcasted_iota(jnp.int32, (TT, TT), 0).astype(F32)
    lane = lax.broadcasted_iota(jnp.int32, (1, TT), 1).astype(F32)
    lane8 = lax.broadcasted_iota(jnp.int32, (8, TT), 1)

    def tile(b, accs):
        lo = off_ref[p * LANES + b]
        t0 = pl.multiple_of(b * TT, TT)
        srow = slot_ref[0, :, pl.ds(t0, TT)]
        g = aff_ref[0, :, pl.ds(t0, TT)]
        g_hi = g.astype(BF16)
        r1 = g - g_hi.astype(F32)
        g_mid = r1.astype(BF16)
        g_lo = (r1 - g_mid.astype(F32)).astype(BF16)
        tile_id = jnp.full((1, TT), b, jnp.int32).astype(F32)
        data = jnp.concatenate([
            tile_id.astype(BF16), lane.astype(BF16), g_hi, g_mid, g_lo,
            kord_ref[0, :, pl.ds(t0, TT)].astype(BF16),
            jnp.zeros((2, TT), BF16)], axis=0)
        onehot = jnp.where(srow - lo.astype(F32) == sub, 1.0, 0.0).astype(BF16)
        res = lax.dot_general(data, onehot, (((1,), (1,)), ((), ())), preferred_element_type=F32)
        shift = lo & (TT - 1)
        j0 = lo // TT
        rolled = pltpu.roll(res, shift, axis=1)
        first = lane8 >= shift
        out = []
        for j in range(nj):
            mine = jnp.logical_or(jnp.logical_and(first, j == j0),
                                  jnp.logical_and(jnp.logical_not(first), j == j0 + 1))
            out.append(accs[j] + jnp.where(mine, rolled, 0.0))
        return tuple(out)

    def tiles(i, accs):
        for u in range(COMPACT_UNROLL):
            accs = tile(i * COMPACT_UNROLL + u, accs)
        return accs
    zero = jnp.zeros((8, TT), F32)
    accs = lax.fori_loop(0, NTT // COMPACT_UNROLL, tiles, (zero,) * nj)
    for j in range(nj):
        o_ref[0, :, j * TT:(j + 1) * TT] = accs[j]


def _compact(offs, slot, aff_rows, kord):
    npair = (NALL // NTOK) * N_EXPERTS
    spec = pl.BlockSpec((1, 1, NTOK), lambda p, off: (p, 0, 0))
    return pl.pallas_call(
        _compact_kernel,
        grid_spec=pltpu.PrefetchScalarGridSpec(
            num_scalar_prefetch=1,
            grid=(npair,),
            in_specs=[spec, spec, spec],
            out_specs=pl.BlockSpec((1, 8, CAP), lambda p, off: (p, 0, 0)),
        ),
        out_shape=jax.ShapeDtypeStruct((npair, 8, CAP), F32),
        compiler_params=_cp(("arbitrary",)),
        name="compact",
    )(offs.reshape(-1), slot.reshape(npair, 1, NTOK), aff_rows.reshape(npair, 1, NTOK),
      kord.reshape(npair, 1, NTOK))


FSPLIT = 2
NPAIR = (NALL // NTOK) * N_EXPERTS
GATHER_UNROLL = 8


def _ffn_kernel(idx_ref, hn_ref, wg_ref, wu_ref, wd_ref, g_ref, o_ref, xs_a, xs_b, acc_ref, sem):
    p = pl.program_id(0)
    f = pl.program_id(1)
    share = CAP // FSPLIT

    def start_row(pair, dst_ref, dst_sem, s):
        src = pl.multiple_of(idx_ref[pair * CAP + s] * RT, RT)
        pltpu.make_async_copy(hn_ref.at[pl.ds(src, RT)], dst_ref.at[pl.ds(s * RT, RT)], dst_sem).start()

    def wait_rows(ref, ref_sem):
        pltpu.make_async_copy(hn_ref.at[pl.ds(0, CAP * RT)], ref, ref_sem).wait()

    @pl.when(jnp.logical_and(p == 0, f == 0))
    def _():
        acc_ref[...] = jnp.zeros_like(acc_ref)

        def issue(i, c):
            for u in range(GATHER_UNROLL):
                start_row(0, xs_a, sem.at[0], i * GATHER_UNROLL + u)
            return c
        lax.fori_loop(0, CAP // GATHER_UNROLL, issue, 0)

    def step(cur, cur_sem, nxt, nxt_sem):
        @pl.when(f == 0)
        def _():
            wait_rows(cur, cur_sem)

        nxt_pair = jnp.minimum(p + 1, NPAIR - 1)
        for s in range(share):
            start_row(nxt_pair, nxt, nxt_sem, f * share + s)

        x = _load_row_tiles(cur, 0, CAP)
        hg = jnp.dot(x, wg_ref[0, 0], preferred_element_type=F32)
        hu = jnp.dot(x, wu_ref[0, 0], preferred_element_type=F32)
        hid = hg * _sigmoid(hg) * hu
        o = jnp.dot(hid, wd_ref[0, 0], preferred_element_type=F32)

        acc = jnp.where(f > 0, acc_ref[...], 0.0) + o
        acc_ref[...] = acc
        _store_row_tiles(o_ref, 0, CAP, acc * g_ref[0][:, 0:1])

        @pl.when(jnp.logical_and(p == NPAIR - 1, f == FSPLIT - 1))
        def _():
            wait_rows(nxt, nxt_sem)

    @pl.when(p % 2 == 0)
    def _():
        step(xs_a, sem.at[0], xs_b, sem.at[1])

    @pl.when(p % 2 == 1)
    def _():
        step(xs_b, sem.at[1], xs_a, sem.at[0])


def _ffn(l, idx_rows, hn, w_gate, w_up, w_down, g_cols):
    fw = D_EXPERT // FSPLIT
    return pl.pallas_call(
        _ffn_kernel,
        grid_spec=pltpu.PrefetchScalarGridSpec(
            num_scalar_prefetch=1,
            grid=(NPAIR, FSPLIT),
            in_specs=[
                pl.BlockSpec(memory_space=pl.ANY),
                pl.BlockSpec((1, 1, D, fw), lambda p, f, idx: (l, p % N_EXPERTS, 0, f)),
                pl.BlockSpec((1, 1, D, fw), lambda p, f, idx: (l, p % N_EXPERTS, 0, f)),
                pl.BlockSpec((1, 1, fw, D), lambda p, f, idx: (l, p % N_EXPERTS, f, 0)),
                pl.BlockSpec((1, CAP, LANES), lambda p, f, idx: (p, 0, 0)),
            ],
            out_specs=pl.BlockSpec((CAP * RT, LANES), lambda p, f, idx: (p, 0)),
            scratch_shapes=[
                pltpu.VMEM((CAP * RT, LANES), F32),
                pltpu.VMEM((CAP * RT, LANES), F32),
                pltpu.VMEM((CAP, D), F32),
                pltpu.SemaphoreType.DMA((2,)),
            ],
        ),
        out_shape=jax.ShapeDtypeStruct((NPAIR * CAP * RT, LANES), F32),
        compiler_params=_cp(("arbitrary", "arbitrary")),
        name="ffn",
    )(idx_rows, hn, w_gate, w_up, w_down, g_cols)


COMBINE_UNROLL = 8
STAGE_ROWS = N_EXPERTS * TT


def _combine_kernel(off_ref, dst_ref, maxk_ref, x1_ref, mod_ref, cnt_ref, gf_ref, eo_ref,
                    octx_ref, o_ref, stage_ref, sem, *, final):
    i = pl.program_id(0)
    ntile = pl.num_programs(0)
    buf = i % 2

    def fetch_tile(tile, slot):
        s = tile // NTT
        b = tile % NTT
        slot_base = slot * STAGE_ROWS

        def fetch(r):
            dst = pl.multiple_of((slot_base + dst_ref[r]) * RT, RT)
            pltpu.make_async_copy(eo_ref.at[pl.ds(pl.multiple_of(r * RT, RT), RT)], stage_ref.at[pl.ds(dst, RT)],
                                  sem.at[slot]).start()

        for e in range(N_EXPERTS):
            pe = s * N_EXPERTS + e
            lo = off_ref[pe * LANES + b]
            n = off_ref[pe * LANES + b + 1] - lo
            base = pe * CAP + lo
            nq = lax.shift_right_logical(n, COMBINE_UNROLL.bit_length() - 1)

            def issue_many(q, c, base=base):
                for u in range(COMBINE_UNROLL):
                    fetch(base + q * COMBINE_UNROLL + u)
                return c
            lax.fori_loop(0, nq, issue_many, 0)

            def issue_one(r, c):
                fetch(r)
                return c
            lax.fori_loop(base + nq * COMBINE_UNROLL, base + n, issue_one, 0)

    def rows_of_tile(tile):
        s = tile // NTT
        b = tile % NTT
        total = 0
        for e in range(N_EXPERTS):
            pe = s * N_EXPERTS + e
            total = total + off_ref[pe * LANES + b + 1] - off_ref[pe * LANES + b]
        return total

    @pl.when(i == 0)
    def _():
        fetch_tile(i, buf)

    @pl.when(i + 1 < ntile)
    def _():
        fetch_tile(i + 1, 1 - buf)

    total = rows_of_tile(i)

    @pl.when(total > 0)
    def _():
        pltpu.make_async_copy(eo_ref.at[pl.ds(0, total * RT)], stage_ref.at[pl.ds(0, total * RT)],
                              sem.at[buf]).wait()

    cnt = cnt_ref[:, 0:1]
    o_ref[...] = jnp.zeros_like(o_ref)

    def add(k, c):
        rows = _load_row_tiles(stage_ref, pl.multiple_of(buf * STAGE_ROWS + k * TT, TT), TT)
        o_ref[...] += jnp.where(cnt > k.astype(F32), rows, 0.0)
        return c
    lax.fori_loop(0, maxk_ref[(i // NTT) * LANES + i % NTT], add, 0)
    x2 = x1_ref[...] + mod_ref[0, 5:6, :] * o_ref[...]
    if final:
        x2 = _rms(x2, gf_ref[...])
    o_ref[...] = x2

    @pl.when(i < NTT)
    def _():
        octx_ref[...] = x2


def _combine(offs, dst_rows, maxk, x1, mod_l, cnt_cols, g_final, eo, final):
    def mrow(i):
        return jnp.where(i < NTT, 0, 1 + (i - NTT) // (LAT_L // TT))
    return pl.pallas_call(
        functools.partial(_combine_kernel, final=final),
        grid_spec=pltpu.PrefetchScalarGridSpec(
            num_scalar_prefetch=3,
            grid=(NALL // TT,),
            in_specs=[
                pl.BlockSpec((TT, D), lambda i, *_: (i, 0)),
                pl.BlockSpec((1, 8, D), lambda i, *_: (mrow(i), 0, 0)),
                pl.BlockSpec((TT, LANES), lambda i, *_: (i, 0)),
                pl.BlockSpec((1, D), lambda i, *_: (0, 0)),
                pl.BlockSpec(memory_space=pl.ANY),
            ],
            out_specs=_stream_specs(TT),
            scratch_shapes=[pltpu.VMEM((2 * STAGE_ROWS * RT, LANES), F32), pltpu.SemaphoreType.DMA((2,))],
        ),
        out_shape=[jax.ShapeDtypeStruct((NTOK, D), F32)] * 2,
        compiler_params=_cp(("arbitrary",)),
        name="combine",
    )(offs.reshape(-1), dst_rows, maxk.reshape(-1), x1, mod_l, cnt_cols, g_final.reshape(1, D), eo)


def _gate_blocks(w_a, w_x):
    hd = w_a.shape[-1]
    per = LANES // hd
    ncb = D_LRU // LANES
    mats = jnp.stack([w_a[0], w_x[0], w_a[1], w_x[1]], axis=0)
    mats = mats.reshape(4, ncb, per, hd, hd)
    out = jnp.zeros((ncb, 4, LANES, LANES), F32)
    for q in range(per):
        out = out.at[:, :, q * hd:(q + 1) * hd, q * hd:(q + 1) * hd].set(mats[:, :, q].transpose(1, 0, 2, 3))
    return out.astype(MM)


def kernel(x_prompt, x_sample, state_rglru, c, c_ctx, w_mod, b_mod, g_norm_mix, g_norm_ffn, w_in, w_pool,
           pool_scale, w_conv, b_conv, w_gate_a, b_gate_a, w_gate_x, b_gate_x, lru_lambda, w_out, w_router,
           w_exp_gate, w_exp_up, w_exp_down, g_final):
    nb_ctx = x_prompt.shape[0]
    nb_lat = x_sample.shape[0]
    xa, xb = x_prompt.reshape(NTOK, D), x_sample.reshape(NTOK, D)

    cvec = jnp.zeros((8, D), F32).at[0].set(c_ctx).at[1:1 + nb_lat].set(c)
    mod = _modulation(cvec, w_mod, b_mod)
    mod = jnp.pad(mod.transpose(0, 2, 1, 3), ((0, 0), (0, 0), (0, 2), (0, 0)))

    nseg = NALL // SEG
    states = []
    for l in range(DEPTH):
        u_pool, u_lru, u_gate = _inproj(xa, xb, mod[l], g_norm_mix[l], w_in[l].astype(MM))
        y_pool = _pool(u_pool, w_pool[l].astype(MM), pool_scale[l])
        h0 = jnp.zeros((nseg, 2, D_LRU), F32).at[NTOK // SEG:].set(state_rglru[:, l])
        bg = jnp.stack([b_gate_a[l, 0], b_gate_x[l, 0], b_gate_a[l, 1], b_gate_x[l, 1]], axis=0).reshape(4, D_LRU)
        y_lru, st = _lru(u_lru, u_gate, w_conv[l], b_conv[l], _gate_blocks(w_gate_a[l], w_gate_x[l]), bg,
                         lru_lambda[l], h0)
        states.append(st[:NTOK // SEG].transpose(0, 2, 1, 3).reshape(nb_ctx, 2, D_LRU))

        wr = w_router[l].T
        wr_hi = wr.astype(BF16)
        wr_split = jnp.stack([wr_hi, (wr - wr_hi.astype(F32)).astype(BF16)], axis=0)
        x1, hn, aff_t = _outproj(xa, xb, y_pool, y_lru, mod[l], g_norm_ffn[l], w_out[l].astype(MM), wr_split)
        slot, kord, cnt, offs, maxk = _plan(aff_t)
        aff_rows = aff_t.reshape(N_EXPERTS, NALL // NTOK, NTOK).transpose(1, 0, 2)
        lists = _compact(offs, slot, aff_rows, kord)
        tok_local = (lists[:, 0] * TT + lists[:, 1]).astype(jnp.int32)
        g_sel = lists[:, 2] + lists[:, 3] + lists[:, 4]
        kord_i = lists[:, 5].astype(jnp.int32)
        stream_base = (jnp.arange(NPAIR, dtype=jnp.int32) // N_EXPERTS) * NTOK
        src_rows = (tok_local + stream_base[:, None]).reshape(-1)
        g_cols = jnp.broadcast_to(g_sel[:, :, None], g_sel.shape + (LANES,))
        eo = _ffn(l, src_rows, hn, w_exp_gate, w_exp_up, w_exp_down, g_cols)
        cnt_cols = jnp.broadcast_to(cnt.reshape(NALL, 1), (NALL, LANES))
        dst_rows = (kord_i * TT + tok_local % TT).reshape(-1)
        xa, xb = _combine(offs, dst_rows, maxk, x1, mod[l], cnt_cols, g_final, eo, final=(l == DEPTH - 1))

    return xa.reshape(x_prompt.shape), xb.reshape(x_sample.shape), jnp.stack(states, axis=1)
```

```python
import functools

import jax
import jax.numpy as jnp
from jax import lax
from jax.experimental import pallas as pl
from jax.experimental.pallas import tpu as pltpu

F32 = jnp.float32
BF16 = jnp.bfloat16
MM = jnp.float32
HIGHEST = lax.Precision.HIGHEST

D = 1024
NTOK = 8192
NALL = 2 * NTOK
CTX_L = 256
LAT_L = 4096
GRID_W = 64
DEPTH = 2
D_POOL = 512
D_LRU = 512
D_IN = D_POOL + 2 * D_LRU
POOL_WINDOWS = (2, 4, 8, 16)
LANES = 128
N_EXPERTS = 16
CAP = 2 * NTOK // N_EXPERTS
D_EXPERT = 1024
LRU_C = 8.0
EPS = 1e-6

TM = 512
SEG = 4096
CH = 256
NCH = SEG // CH
CHS = CH + 8
TT = 256
NTT = NTOK // TT
VMEM_LIMIT = 56 * 1024 * 1024


def _cp(sem, vmem=VMEM_LIMIT):
    return pltpu.CompilerParams(dimension_semantics=sem, vmem_limit_bytes=vmem)


def _sigmoid(x):
    return 0.5 * jnp.tanh(0.5 * x) + 0.5


def _mod_row(i):
    n_ctx = NTOK // TM
    return jnp.where(i < n_ctx, 0, 1 + (i - n_ctx) // (LAT_L // TM))


def _mod_kernel(c_ref, w_ref, b_ref, o_ref):
    cv = c_ref[...]
    s = cv * jax.nn.sigmoid(cv)
    o_ref[0, 0] = jnp.dot(s, w_ref[0], precision=HIGHEST, preferred_element_type=F32) + b_ref[0, 0]


def _modulation(cvec, w_mod, b_mod):
    return pl.pallas_call(
        _mod_kernel,
        grid=(DEPTH, 6),
        in_specs=[
            pl.BlockSpec((8, D), lambda l, j: (0, 0)),
            pl.BlockSpec((1, D, D), lambda l, j: (l, 0, j)),
            pl.BlockSpec((1, 1, 1, D), lambda l, j: (l, j, 0, 0)),
        ],
        out_specs=pl.BlockSpec((1, 1, 8, D), lambda l, j: (l, j, 0, 0)),
        out_shape=jax.ShapeDtypeStruct((DEPTH, 6, 8, D), F32),
        compiler_params=_cp(("arbitrary", "arbitrary")),
        name="modulation",
    )(cvec, w_mod, b_mod.reshape(DEPTH, 6, 1, D))


def _rms(x, g):
    return x * lax.rsqrt(jnp.mean(x * x, axis=-1, keepdims=True) + EPS) * g


def _stream_specs(tile):
    n = NTOK // tile
    return [pl.BlockSpec((tile, D), lambda i, *_: (jnp.minimum(i, n - 1), 0)),
            pl.BlockSpec((tile, D), lambda i, *_: (jnp.maximum(i - n, 0), 0))]


def _stream_tile(xa_ref, xb_ref, tile):
    return jnp.where(pl.program_id(0) < NTOK // tile, xa_ref[...], xb_ref[...])


def _inproj_kernel(xa_ref, xb_ref, mod_ref, g_ref, w_ref, up_ref, ul_ref, ug_ref):
    x = _stream_tile(xa_ref, xb_ref, TM)
    hn = _rms(x, g_ref[...]) * (1.0 + mod_ref[0, 1:2, :]) + mod_ref[0, 0:1, :]
    proj = jnp.dot(hn.astype(MM), w_ref[...], preferred_element_type=F32)
    up_ref[...] = proj[:, :D_POOL]
    ul_ref[...] = proj[:, D_POOL:D_POOL + D_LRU]
    ug_ref[...] = proj[:, D_POOL + D_LRU:]


def _inproj(xa, xb, mod_l, g, w_in_bf):
    out = jax.ShapeDtypeStruct((NALL, D_POOL), F32)
    return pl.pallas_call(
        _inproj_kernel,
        grid=(NALL // TM,),
        in_specs=_stream_specs(TM) + [
            pl.BlockSpec((1, 8, D), lambda i: (_mod_row(i), 0, 0)),
            pl.BlockSpec((1, D), lambda i: (0, 0)),
            pl.BlockSpec((D, D_IN), lambda i: (0, 0)),
        ],
        out_specs=[pl.BlockSpec((TM, D_POOL), lambda i: (i, 0))] * 3,
        out_shape=[out, out, out],
        compiler_params=_cp(("arbitrary",)),
        name="inproj",
    )(xa, xb, mod_l, g.reshape(1, D), w_in_bf)


def _window_sum(x, h, pos, period):
    n = x.shape[0]
    acc = x
    for i in range(1, h):
        acc = acc + jnp.where(pos < period - i, pltpu.roll(x, n - i, axis=0), 0.0)
    for i in range(1, h + 1):
        acc = acc + jnp.where(pos >= i, pltpu.roll(x, i, axis=0), 0.0)
    return acc


def _window_count(pos, h, period):
    return (jnp.minimum(pos + h, period) - jnp.maximum(pos - h, 0)).astype(F32)


def _pool_kernel(u_ref, w_ref, s_ref, o_ref, pad_ref):
    seg = pl.program_id(0)
    row = lax.broadcasted_iota(jnp.int32, (CH, LANES), 0)
    halo = 8 * GRID_W

    def project(d, g, t0):
        y = jnp.dot(d.astype(MM), w_ref[g], preferred_element_type=F32)
        y = y * s_ref[:, g * LANES:(g + 1) * LANES]
        o_ref[pl.ds(t0, CH), g * LANES:(g + 1) * LANES] = y.astype(MM)

    @pl.when(seg < NTOK // SEG)
    def _context():
        def body(sb, c):
            t0 = pl.multiple_of(sb * CH, CH)
            for g, w in enumerate(POOL_WINDOWS):
                h = w // 2
                x = u_ref[pl.ds(t0, CH), g * LANES:(g + 1) * LANES]
                m = _window_sum(x, h, row, CTX_L) / _window_count(row, h, CTX_L)
                project(m - x, g, t0)
            return c
        lax.fori_loop(0, NCH, body, 0)

    @pl.when(seg >= NTOK // SEG)
    def _latent():
        zeros = jnp.zeros((halo, D_POOL), F32)
        pad_ref[0:halo, :] = zeros
        pad_ref[halo + SEG:halo + SEG + halo, :] = zeros
        col = row & (GRID_W - 1)

        def cols(sb, c):
            t0 = pl.multiple_of(sb * CH, CH)
            for g, w in enumerate(POOL_WINDOWS):
                h = w // 2
                x = u_ref[pl.ds(t0, CH), g * LANES:(g + 1) * LANES]
                m = _window_sum(x, h, col, GRID_W) / _window_count(col, h, GRID_W)
                pad_ref[pl.ds(halo + t0, CH), g * LANES:(g + 1) * LANES] = m
            return c
        lax.fori_loop(0, NCH, cols, 0)

        def rows(sb, c):
            t0 = pl.multiple_of(sb * CH, CH)
            r = jnp.right_shift(t0 + row, 6)
            for g, w in enumerate(POOL_WINDOWS):
                h = w // 2
                acc = None
                for i in range(-h, h):
                    start = pl.multiple_of(halo + t0 + i * GRID_W, GRID_W)
                    v = pad_ref[pl.ds(start, CH), g * LANES:(g + 1) * LANES]
                    acc = v if acc is None else acc + v
                m = acc / _window_count(r, h, GRID_W)
                x = u_ref[pl.ds(t0, CH), g * LANES:(g + 1) * LANES]
                project(m - x, g, t0)
            return c
        lax.fori_loop(0, NCH, rows, 0)


def _pool(u_pool, w_pool_bf, pool_scale):
    halo = 8 * GRID_W
    return pl.pallas_call(
        _pool_kernel,
        grid=(NALL // SEG,),
        in_specs=[
            pl.BlockSpec((SEG, D_POOL), lambda i: (i, 0)),
            pl.BlockSpec((len(POOL_WINDOWS), LANES, LANES), lambda i: (0, 0, 0)),
            pl.BlockSpec((1, D_POOL), lambda i: (0, 0)),
        ],
        out_specs=pl.BlockSpec((SEG, D_POOL), lambda i: (i, 0)),
        out_shape=jax.ShapeDtypeStruct((NALL, D_POOL), MM),
        scratch_shapes=[pltpu.VMEM((SEG + 2 * halo, D_POOL), F32)],
        compiler_params=_cp(("arbitrary",)),
        name="pool",
    )(u_pool, w_pool_bf, pool_scale.reshape(1, D_POOL))


def _lru_kernel(ul_ref, ug_ref, wc_ref, bc_ref, wg_ref, bg_ref, lam_ref, h0_ref,
                o_ref, st_ref, xc_ref, a_ref, b_ref, hin_ref):
    seg = pl.program_id(0)
    is_lat = seg >= NTOK // SEG
    last = jnp.where(is_lat, LAT_L - 1, CTX_L - 1)

    x = ul_ref[...]
    pos = lax.broadcasted_iota(jnp.int32, (SEG, LANES), 0) & last
    xc = bc_ref[...] + wc_ref[0:1, :] * jnp.where(pos >= 2, pltpu.roll(x, 2, axis=0), 0.0)
    xc = xc + wc_ref[1:2, :] * jnp.where(pos >= 1, pltpu.roll(x, 1, axis=0), 0.0)
    xc = xc + wc_ref[2:3, :] * x
    xc = xc + wc_ref[3:4, :] * jnp.where(pos < last, pltpu.roll(x, SEG - 1, axis=0), 0.0)
    xc_ref[...] = xc

    def gates(c, carry):
        t0 = pl.multiple_of(c * CH, CH)
        xcb = xc_ref[pl.ds(t0, CH), :]
        xb = xcb.astype(MM)
        hx = 0.5 * xcb
        for d in range(2):
            tr = jnp.tanh(jnp.dot(xb, wg_ref[0, 2 * d], preferred_element_type=F32) + bg_ref[2 * d:2 * d + 1, :])
            ti = jnp.tanh(jnp.dot(xb, wg_ref[0, 2 * d + 1], preferred_element_type=F32)
                          + bg_ref[2 * d + 1:2 * d + 2, :])
            half_c = (-0.5 * LRU_C) * jax.nn.softplus(-lam_ref[d:d + 1, :])
            log_a = half_c * tr + half_c
            ix = hx * ti + hx
            a = jnp.exp(log_a)
            m2 = jnp.maximum(-jnp.tanh(log_a) * (a * a + 1.0), 0.0)
            mult = jnp.where(m2 > 0.0, m2 * lax.rsqrt(m2), 0.0)
            s0 = pl.multiple_of(c * CHS, 8)
            a_ref[d, pl.ds(s0, CH), :] = a
            b_ref[d, pl.ds(s0, CH), :] = mult * ix
        return carry
    lax.fori_loop(0, NCH, gates, 0)

    def scan(j, carry):
        hf, pf, hb, pb = carry
        jb = CH - 1 - j
        sf = pl.ds(j, NCH, stride=CHS)
        sb = pl.ds(jb, NCH, stride=CHS)
        af = a_ref[0, sf, :]
        bf = b_ref[0, sf, :]
        ab = a_ref[1, sb, :]
        bb = b_ref[1, sb, :]
        hf = af * hf + bf
        pf = af * pf
        hb = ab * hb + bb
        pb = ab * pb
        b_ref[0, sf, :] = hf
        a_ref[0, sf, :] = pf
        b_ref[1, sb, :] = hb
        a_ref[1, sb, :] = pb
        return hf, pf, hb, pb
    z = jnp.zeros((NCH, LANES), F32)
    o = jnp.ones((NCH, LANES), F32)
    hf_end, pf_end, hb_beg, pb_beg = lax.fori_loop(0, CH, scan, (z, o, z, o))

    cur = h0_ref[0, 0:1, :]
    rows_f = []
    for c in range(NCH):
        rows_f.append(cur)
        cur = pf_end[c:c + 1, :] * cur + hf_end[c:c + 1, :]
    cur = h0_ref[0, 1:2, :]
    rows_b = [None] * NCH
    for c in range(NCH - 1, -1, -1):
        rows_b[c] = cur
        cur = pb_beg[c:c + 1, :] * cur + hb_beg[c:c + 1, :]
    hin_f = jnp.where(is_lat, jnp.concatenate(rows_f, axis=0), 0.0)
    hin_b = jnp.where(is_lat, jnp.concatenate(rows_b, axis=0), 0.0)
    hin_ref[0] = hin_f
    hin_ref[1] = hin_b
    st_ref[0, 0] = pf_end * hin_f + hf_end
    st_ref[0, 1] = pb_beg * hin_b + hb_beg

    def finish(c, carry):
        t0 = pl.multiple_of(c * CH, CH)
        rows = pl.ds(pl.multiple_of(c * CHS, 8), CH)
        hf = a_ref[0, rows, :] * hin_ref[0, pl.ds(c, 1), :] + b_ref[0, rows, :]
        hb = a_ref[1, rows, :] * hin_ref[1, pl.ds(c, 1), :] + b_ref[1, rows, :]
        y = jax.nn.gelu(ug_ref[pl.ds(t0, CH), :]) * (hf + hb)
        o_ref[pl.ds(t0, CH), :] = y.astype(MM)
        return carry
    lax.fori_loop(0, NCH, finish, 0)


def _lru(u_lru, u_gate, w_conv, b_conv, wg_bd, bg, lam, h0):
    nseg = NALL // SEG
    ncb = D_LRU // LANES
    return pl.pallas_call(
        _lru_kernel,
        grid=(nseg, ncb),
        in_specs=[
            pl.BlockSpec((SEG, LANES), lambda s, c: (s, c)),
            pl.BlockSpec((SEG, LANES), lambda s, c: (s, c)),
            pl.BlockSpec((4, LANES), lambda s, c: (0, c)),
            pl.BlockSpec((1, LANES), lambda s, c: (0, c)),
            pl.BlockSpec((1, 4, LANES, LANES), lambda s, c: (c, 0, 0, 0)),
            pl.BlockSpec((4, LANES), lambda s, c: (0, c)),
            pl.BlockSpec((2, LANES), lambda s, c: (0, c)),
            pl.BlockSpec((1, 2, LANES), lambda s, c: (s, 0, c)),
        ],
        out_specs=[
            pl.BlockSpec((SEG, LANES), lambda s, c: (s, c)),
            pl.BlockSpec((1, 2, NCH, LANES), lambda s, c: (s, 0, 0, c)),
        ],
        out_shape=[
            jax.ShapeDtypeStruct((NALL, D_LRU), MM),
            jax.ShapeDtypeStruct((nseg, 2, NCH, D_LRU), F32),
        ],
        scratch_shapes=[
            pltpu.VMEM((SEG, LANES), F32),
            pltpu.VMEM((2, NCH * CHS, LANES), F32),
            pltpu.VMEM((2, NCH * CHS, LANES), F32),
            pltpu.VMEM((2, NCH, LANES), F32),
        ],
        compiler_params=_cp(("arbitrary", "arbitrary")),
        name="lru",
    )(u_lru, u_gate, w_conv, b_conv.reshape(1, D_LRU), wg_bd, bg, lam, h0)


RT = D // LANES


def _store_row_tiles(ref, start, n, v):
    for c in range(RT):
        ref[pl.ds(start * RT + c, n, stride=RT), :] = v[:, c * LANES:(c + 1) * LANES]


def _load_row_tiles(ref, start, n):
    return jnp.concatenate([ref[pl.ds(start * RT + c, n, stride=RT), :] for c in range(RT)], axis=1)


def _outproj_kernel(xa_ref, xb_ref, yp_ref, yl_ref, mod_ref, g_ref, wo_ref, wr_ref, x1_ref, hn_ref, aff_ref):
    mix = jnp.dot(yp_ref[...], wo_ref[0:D_POOL, :], preferred_element_type=F32)
    mix = mix + jnp.dot(yl_ref[...], wo_ref[D_POOL:, :], preferred_element_type=F32)
    x1 = _stream_tile(xa_ref, xb_ref, TM) + mod_ref[0, 2:3, :] * mix
    x1_ref[...] = x1
    hn = _rms(x1, g_ref[...]) * (1.0 + mod_ref[0, 4:5, :]) + mod_ref[0, 3:4, :]
    _store_row_tiles(hn_ref, 0, TM, hn)
    hn_hi = hn.astype(BF16)
    hn_lo = (hn - hn_hi.astype(F32)).astype(BF16)
    nt = (((1,), (1,)), ((), ()))
    logits = (lax.dot_general(wr_ref[0], hn_hi, nt, preferred_element_type=F32)
              + lax.dot_general(wr_ref[0], hn_lo, nt, preferred_element_type=F32)
              + lax.dot_general(wr_ref[1], hn_hi, nt, preferred_element_type=F32))
    e = jnp.exp(logits - jnp.max(logits, axis=0, keepdims=True))
    aff_ref[...] = e / jnp.sum(e, axis=0, keepdims=True)


def _outproj(xa, xb, y_pool, y_lru, mod_l, g, w_out_bf, w_router_t):
    return pl.pallas_call(
        _outproj_kernel,
        grid=(NALL // TM,),
        in_specs=_stream_specs(TM) + [
            pl.BlockSpec((TM, D_POOL), lambda i: (i, 0)),
            pl.BlockSpec((TM, D_LRU), lambda i: (i, 0)),
            pl.BlockSpec((1, 8, D), lambda i: (_mod_row(i), 0, 0)),
            pl.BlockSpec((1, D), lambda i: (0, 0)),
            pl.BlockSpec((D, D), lambda i: (0, 0)),
            pl.BlockSpec((2, N_EXPERTS, D), lambda i: (0, 0, 0)),
        ],
        out_specs=[
            pl.BlockSpec((TM, D), lambda i: (i, 0)),
            pl.BlockSpec((TM * RT, LANES), lambda i: (i, 0)),
            pl.BlockSpec((N_EXPERTS, TM), lambda i: (0, i)),
        ],
        out_shape=[
            jax.ShapeDtypeStruct((NALL, D), F32),
            jax.ShapeDtypeStruct((NALL * RT, LANES), F32),
            jax.ShapeDtypeStruct((N_EXPERTS, NALL), F32),
        ],
        compiler_params=_cp(("arbitrary",)),
        name="outproj",
    )(xa, xb, y_pool, y_lru, mod_l, g.reshape(1, D), w_out_bf, w_router_t)


def _lane_prefix(m, tri):
    parts = []
    bounds = []
    run = jnp.zeros((m.shape[0], 1), F32)
    for k in range(NTOK // LANES):
        if (k * LANES) % TT == 0:
            bounds.append(run)
        p = jnp.dot(m[:, k * LANES:(k + 1) * LANES].astype(BF16), tri, preferred_element_type=F32)
        parts.append(p + run)
        run = run + p[:, LANES - 1:LANES]
    bounds.append(run)
    return jnp.concatenate(parts, axis=1), bounds


def _plan_kernel(aff_ref, slot_ref, kord_ref, cnt_ref, off_ref, maxk_ref):
    aff = aff_ref[...]
    bits = pltpu.bitcast(aff, jnp.int32)

    def count_ge(v):
        return jnp.sum(jnp.where(bits >= v, 1.0, 0.0), axis=1, keepdims=True)

    def bisect(_, c):
        lo, hi = c
        mid = lo + jnp.right_shift(hi - lo + 1, 1)
        ok = count_ge(mid) >= float(CAP)
        return jnp.where(ok, mid, lo), jnp.where(ok, hi, mid - 1)
    lo0 = jnp.zeros((N_EXPERTS, 1), jnp.int32)
    hi0 = jnp.full((N_EXPERTS, 1), 0x7F7FFFFF, jnp.int32)
    thr, _ = lax.fori_loop(0, 31, bisect, (lo0, hi0))

    ri = lax.broadcasted_iota(jnp.int32, (LANES, LANES), 0)
    ci = lax.broadcasted_iota(jnp.int32, (LANES, LANES), 1)
    tri = jnp.where(ri <= ci, 1.0, 0.0).astype(BF16)

    gt = bits > thr
    eq = bits == thr
    need = float(CAP) - jnp.sum(jnp.where(gt, 1.0, 0.0), axis=1, keepdims=True)
    eq_rank, _ = _lane_prefix(jnp.where(eq, 1.0, 0.0), tri)
    sel = jnp.where(gt, 1.0, jnp.where(eq & (eq_rank <= need), 1.0, 0.0))
    rank, bounds = _lane_prefix(sel, tri)
    slot_ref[0] = jnp.where(sel > 0.0, rank - 1.0, -1.0)

    er = lax.broadcasted_iota(jnp.int32, (N_EXPERTS, N_EXPERTS), 0)
    ec = lax.broadcasted_iota(jnp.int32, (N_EXPERTS, N_EXPERTS), 1)
    below = jnp.where(ec < er, 1.0, 0.0).astype(BF16)
    kord_ref[0] = jnp.dot(below, sel.astype(BF16), preferred_element_type=F32)
    cnt = jnp.sum(sel, axis=0, keepdims=True)
    cnt_ref[0] = cnt

    lane = lax.broadcasted_iota(jnp.int32, (N_EXPERTS, LANES), 1)
    offs = jnp.zeros((N_EXPERTS, LANES), F32)
    for b, col in enumerate(bounds):
        offs = jnp.where(lane == b, col, offs)
    off_ref[0] = offs.astype(jnp.int32)

    lane1 = lax.broadcasted_iota(jnp.int32, (1, LANES), 1)
    mk = jnp.zeros((1, LANES), F32)
    for b in range(NTT):
        mk = jnp.where(lane1 == b, jnp.max(cnt[:, b * TT:(b + 1) * TT], axis=1, keepdims=True), mk)
    maxk_ref[0] = mk.astype(jnp.int32)


def _plan(aff_t):
    ns = NALL // NTOK
    return pl.pallas_call(
        _plan_kernel,
        grid=(ns,),
        in_specs=[pl.BlockSpec((N_EXPERTS, NTOK), lambda s: (0, s))],
        out_specs=[
            pl.BlockSpec((1, N_EXPERTS, NTOK), lambda s: (s, 0, 0)),
            pl.BlockSpec((1, N_EXPERTS, NTOK), lambda s: (s, 0, 0)),
            pl.BlockSpec((1, 1, NTOK), lambda s: (s, 0, 0)),
            pl.BlockSpec((1, N_EXPERTS, LANES), lambda s: (s, 0, 0)),
            pl.BlockSpec((1, 1, LANES), lambda s: (s, 0, 0)),
        ],
        out_shape=[
            jax.ShapeDtypeStruct((ns, N_EXPERTS, NTOK), F32),
            jax.ShapeDtypeStruct((ns, N_EXPERTS, NTOK), F32),
            jax.ShapeDtypeStruct((ns, 1, NTOK), F32),
            jax.ShapeDtypeStruct((ns, N_EXPERTS, LANES), jnp.int32),
            jax.ShapeDtypeStruct((ns, 1, LANES), jnp.int32),
        ],
        compiler_params=_cp(("arbitrary",)),
        name="plan",
    )(aff_t)


COMPACT_UNROLL = 8


def _compact_kernel(off_ref, slot_ref, aff_ref, kord_ref, o_ref):
    p = pl.program_id(0)
    nj = CAP // TT
    sub = lax.broadcasted_iota(jnp.int32, (TT, TT), 0).astype(F32)
    lane = lax.broadcasted_iota(jnp.int32, (1, TT), 1).astype(F32)
    lane8 = lax.broadcasted_iota(jnp.int32, (8, TT), 1)

    def tile(b, accs):
        lo = off_ref[p * LANES + b]
        t0 = pl.multiple_of(b * TT, TT)
        srow = slot_ref[0, :, pl.ds(t0, TT)]
        g = aff_ref[0, :, pl.ds(t0, TT)]
        g_hi = g.astype(BF16)
        r1 = g - g_hi.astype(F32)
        g_mid = r1.astype(BF16)
        g_lo = (r1 - g_mid.astype(F32)).astype(BF16)
        tile_id = jnp.full((1, TT), b, jnp.int32).astype(F32)
        data = jnp.concatenate([
            tile_id.astype(BF16), lane.astype(BF16), g_hi, g_mid, g_lo,
            kord_ref[0, :, pl.ds(t0, TT)].astype(BF16),
            jnp.zeros((2, TT), BF16)], axis=0)
        onehot = jnp.where(srow - lo.astype(F32) == sub, 1.0, 0.0).astype(BF16)
        res = lax.dot_general(data, onehot, (((1,), (1,)), ((), ())), preferred_element_type=F32)
        shift = lo & (TT - 1)
        j0 = lo // TT
        rolled = pltpu.roll(res, shift, axis=1)
        first = lane8 >= shift
        out = []
        for j in range(nj):
            mine = jnp.logical_or(jnp.logical_and(first, j == j0),
                                  jnp.logical_and(jnp.logical_not(first), j == j0 + 1))
            out.append(accs[j] + jnp.where(mine, rolled, 0.0))
        return tuple(out)

    def tiles(i, accs):
        for u in range(COMPACT_UNROLL):
            accs = tile(i * COMPACT_UNROLL + u, accs)
        return accs
    zero = jnp.zeros((8, TT), F32)
    accs = lax.fori_loop(0, NTT // COMPACT_UNROLL, tiles, (zero,) * nj)
    for j in range(nj):
        o_ref[0, :, j * TT:(j + 1) * TT] = accs[j]


def _compact(offs, slot, aff_rows, kord):
    npair = (NALL // NTOK) * N_EXPERTS
    spec = pl.BlockSpec((1, 1, NTOK), lambda p, off: (p, 0, 0))
    return pl.pallas_call(
        _compact_kernel,
        grid_spec=pltpu.PrefetchScalarGridSpec(
            num_scalar_prefetch=1,
            grid=(npair,),
            in_specs=[spec, spec, spec],
            out_specs=pl.BlockSpec((1, 8, CAP), lambda p, off: (p, 0, 0)),
        ),
        out_shape=jax.ShapeDtypeStruct((npair, 8, CAP), F32),
        compiler_params=_cp(("arbitrary",)),
        name="compact",
    )(offs.reshape(-1), slot.reshape(npair, 1, NTOK), aff_rows.reshape(npair, 1, NTOK),
      kord.reshape(npair, 1, NTOK))


FSPLIT = 2
NPAIR = (NALL // NTOK) * N_EXPERTS
GATHER_UNROLL = 8


def _ffn_kernel(idx_ref, hn_ref, wg_ref, wu_ref, wd_ref, g_ref, o_ref, xs_a, xs_b, acc_ref, sem):
    p = pl.program_id(0)
    f = pl.program_id(1)
    share = CAP // FSPLIT

    def start_row(pair, dst_ref, dst_sem, s):
        src = pl.multiple_of(idx_ref[pair * CAP + s] * RT, RT)
        pltpu.make_async_copy(hn_ref.at[pl.ds(src, RT)], dst_ref.at[pl.ds(s * RT, RT)], dst_sem).start()

    def wait_rows(ref, ref_sem):
        pltpu.make_async_copy(hn_ref.at[pl.ds(0, CAP * RT)], ref, ref_sem).wait()

    @pl.when(jnp.logical_and(p == 0, f == 0))
    def _():
        acc_ref[...] = jnp.zeros_like(acc_ref)

        def issue(i, c):
            for u in range(GATHER_UNROLL):
                start_row(0, xs_a, sem.at[0], i * GATHER_UNROLL + u)
            return c
        lax.fori_loop(0, CAP // GATHER_UNROLL, issue, 0)

    def step(cur, cur_sem, nxt, nxt_sem):
        @pl.when(f == 0)
        def _():
            wait_rows(cur, cur_sem)

        nxt_pair = jnp.minimum(p + 1, NPAIR - 1)
        for s in range(share):
            start_row(nxt_pair, nxt, nxt_sem, f * share + s)

        x = _load_row_tiles(cur, 0, CAP)
        hg = jnp.dot(x, wg_ref[0, 0], preferred_element_type=F32)
        hu = jnp.dot(x, wu_ref[0, 0], preferred_element_type=F32)
        hid = hg * _sigmoid(hg) * hu
        o = jnp.dot(hid, wd_ref[0, 0], preferred_element_type=F32)

        acc = jnp.where(f > 0, acc_ref[...], 0.0) + o
        acc_ref[...] = acc
        _store_row_tiles(o_ref, 0, CAP, acc * g_ref[0][:, 0:1])

        @pl.when(jnp.logical_and(p == NPAIR - 1, f == FSPLIT - 1))
        def _():
            wait_rows(nxt, nxt_sem)

    @pl.when(p % 2 == 0)
    def _():
        step(xs_a, sem.at[0], xs_b, sem.at[1])

    @pl.when(p % 2 == 1)
    def _():
        step(xs_b, sem.at[1], xs_a, sem.at[0])


def _ffn(l, idx_rows, hn, w_gate, w_up, w_down, g_cols):
    fw = D_EXPERT // FSPLIT
    return pl.pallas_call(
        _ffn_kernel,
        grid_spec=pltpu.PrefetchScalarGridSpec(
            num_scalar_prefetch=1,
            grid=(NPAIR, FSPLIT),
            in_specs=[
                pl.BlockSpec(memory_space=pl.ANY),
                pl.BlockSpec((1, 1, D, fw), lambda p, f, idx: (l, p % N_EXPERTS, 0, f)),
                pl.BlockSpec((1, 1, D, fw), lambda p, f, idx: (l, p % N_EXPERTS, 0, f)),
                pl.BlockSpec((1, 1, fw, D), lambda p, f, idx: (l, p % N_EXPERTS, f, 0)),
                pl.BlockSpec((1, CAP, LANES), lambda p, f, idx: (p, 0, 0)),
            ],
            out_specs=pl.BlockSpec((CAP * RT, LANES), lambda p, f, idx: (p, 0)),
            scratch_shapes=[
                pltpu.VMEM((CAP * RT, LANES), F32),
                pltpu.VMEM((CAP * RT, LANES), F32),
                pltpu.VMEM((CAP, D), F32),
                pltpu.SemaphoreType.DMA((2,)),
            ],
        ),
        out_shape=jax.ShapeDtypeStruct((NPAIR * CAP * RT, LANES), F32),
        compiler_params=_cp(("arbitrary", "arbitrary")),
        name="ffn",
    )(idx_rows, hn, w_gate, w_up, w_down, g_cols)


COMBINE_UNROLL = 8
STAGE_ROWS = N_EXPERTS * TT


def _combine_kernel(off_ref, dst_ref, maxk_ref, x1_ref, mod_ref, cnt_ref, gf_ref, eo_ref,
                    octx_ref, o_ref, stage_ref, sem, *, final):
    i = pl.program_id(0)
    ntile = pl.num_programs(0)
    buf = i % 2

    def fetch_tile(tile, slot):
        s = tile // NTT
        b = tile % NTT
        slot_base = slot * STAGE_ROWS

        def fetch(r):
            dst = pl.multiple_of((slot_base + dst_ref[r]) * RT, RT)
            pltpu.make_async_copy(eo_ref.at[pl.ds(pl.multiple_of(r * RT, RT), RT)], stage_ref.at[pl.ds(dst, RT)],
                                  sem.at[slot]).start()

        for e in range(N_EXPERTS):
            pe = s * N_EXPERTS + e
            lo = off_ref[pe * LANES + b]
            n = off_ref[pe * LANES + b + 1] - lo
            base = pe * CAP + lo
            nq = lax.shift_right_logical(n, COMBINE_UNROLL.bit_length() - 1)

            def issue_many(q, c, base=base):
                for u in range(COMBINE_UNROLL):
                    fetch(base + q * COMBINE_UNROLL + u)
                return c
            lax.fori_loop(0, nq, issue_many, 0)

            def issue_one(r, c):
                fetch(r)
                return c
            lax.fori_loop(base + nq * COMBINE_UNROLL, base + n, issue_one, 0)

    def rows_of_tile(tile):
        s = tile // NTT
        b = tile % NTT
        total = 0
        for e in range(N_EXPERTS):
            pe = s * N_EXPERTS + e
            total = total + off_ref[pe * LANES + b + 1] - off_ref[pe * LANES + b]
        return total

    @pl.when(i == 0)
    def _():
        fetch_tile(i, buf)

    @pl.when(i + 1 < ntile)
    def _():
        fetch_tile(i + 1, 1 - buf)

    total = rows_of_tile(i)

    @pl.when(total > 0)
    def _():
        pltpu.make_async_copy(eo_ref.at[pl.ds(0, total * RT)], stage_ref.at[pl.ds(0, total * RT)],
                              sem.at[buf]).wait()

    cnt = cnt_ref[:, 0:1]
    o_ref[...] = jnp.zeros_like(o_ref)

    def add(k, c):
        rows = _load_row_tiles(stage_ref, pl.multiple_of(buf * STAGE_ROWS + k * TT, TT), TT)
        o_ref[...] += jnp.where(cnt > k.astype(F32), rows, 0.0)
        return c
    lax.fori_loop(0, maxk_ref[(i // NTT) * LANES + i % NTT], add, 0)
    x2 = x1_ref[...] + mod_ref[0, 5:6, :] * o_ref[...]
    if final:
        x2 = _rms(x2, gf_ref[...])
    o_ref[...] = x2

    @pl.when(i < NTT)
    def _():
        octx_ref[...] = x2


def _combine(offs, dst_rows, maxk, x1, mod_l, cnt_cols, g_final, eo, final):
    def mrow(i):
        return jnp.where(i < NTT, 0, 1 + (i - NTT) // (LAT_L // TT))
    return pl.pallas_call(
        functools.partial(_combine_kernel, final=final),
        grid_spec=pltpu.PrefetchScalarGridSpec(
            num_scalar_prefetch=3,
            grid=(NALL // TT,),
            in_specs=[
                pl.BlockSpec((TT, D), lambda i, *_: (i, 0)),
                pl.BlockSpec((1, 8, D), lambda i, *_: (mrow(i), 0, 0)),
                pl.BlockSpec((TT, LANES), lambda i, *_: (i, 0)),
                pl.BlockSpec((1, D), lambda i, *_: (0, 0)),
                pl.BlockSpec(memory_space=pl.ANY),
            ],
            out_specs=_stream_specs(TT),
            scratch_shapes=[pltpu.VMEM((2 * STAGE_ROWS * RT, LANES), F32), pltpu.SemaphoreType.DMA((2,))],
        ),
        out_shape=[jax.ShapeDtypeStruct((NTOK, D), F32)] * 2,
        compiler_params=_cp(("arbitrary",)),
        name="combine",
    )(offs.reshape(-1), dst_rows, maxk.reshape(-1), x1, mod_l, cnt_cols, g_final.reshape(1, D), eo)


def _gate_blocks(w_a, w_x):
    hd = w_a.shape[-1]
    per = LANES // hd
    ncb = D_LRU // LANES
    mats = jnp.stack([w_a[0], w_x[0], w_a[1], w_x[1]], axis=0)
    mats = mats.reshape(4, ncb, per, hd, hd)
    out = jnp.zeros((ncb, 4, LANES, LANES), F32)
    for q in range(per):
        out = out.at[:, :, q * hd:(q + 1) * hd, q * hd:(q + 1) * hd].set(mats[:, :, q].transpose(1, 0, 2, 3))
    return out.astype(MM)


def kernel(x_prompt, x_sample, state_rglru, c, c_ctx, w_mod, b_mod, g_norm_mix, g_norm_ffn, w_in, w_pool,
           pool_scale, w_conv, b_conv, w_gate_a, b_gate_a, w_gate_x, b_gate_x, lru_lambda, w_out, w_router,
           w_exp_gate, w_exp_up, w_exp_down, g_final):
    nb_ctx = x_prompt.shape[0]
    nb_lat = x_sample.shape[0]
    xa, xb = x_prompt.reshape(NTOK, D), x_sample.reshape(NTOK, D)

    cvec = jnp.zeros((8, D), F32).at[0].set(c_ctx).at[1:1 + nb_lat].set(c)
    mod = _modulation(cvec, w_mod, b_mod)
    mod = jnp.pad(mod.transpose(0, 2, 1, 3), ((0, 0), (0, 0), (0, 2), (0, 0)))

    nseg = NALL // SEG
    states = []
    for l in range(DEPTH):
        u_pool, u_lru, u_gate = _inproj(xa, xb, mod[l], g_norm_mix[l], w_in[l].astype(MM))
        y_pool = _pool(u_pool, w_pool[l].astype(MM), pool_scale[l])
        h0 = jnp.zeros((nseg, 2, D_LRU), F32).at[NTOK // SEG:].set(state_rglru[:, l])
        bg = jnp.stack([b_gate_a[l, 0], b_gate_x[l, 0], b_gate_a[l, 1], b_gate_x[l, 1]], axis=0).reshape(4, D_LRU)
        y_lru, st = _lru(u_lru, u_gate, w_conv[l], b_conv[l], 0.5 * _gate_blocks(w_gate_a[l], w_gate_x[l]),
                         0.5 * bg, lru_lambda[l], h0)
        states.append(st[:NTOK // SEG].transpose(0, 2, 1, 3).reshape(nb_ctx, 2, D_LRU))

        wr = w_router[l].T
        wr_hi = wr.astype(BF16)
        wr_split = jnp.stack([wr_hi, (wr - wr_hi.astype(F32)).astype(BF16)], axis=0)
        x1, hn, aff_t = _outproj(xa, xb, y_pool, y_lru, mod[l], g_norm_ffn[l], w_out[l].astype(MM), wr_split)
        slot, kord, cnt, offs, maxk = _plan(aff_t)
        aff_rows = aff_t.reshape(N_EXPERTS, NALL // NTOK, NTOK).transpose(1, 0, 2)
        lists = _compact(offs, slot, aff_rows, kord)
        tok_local = (lists[:, 0] * TT + lists[:, 1]).astype(jnp.int32)
        g_sel = lists[:, 2] + lists[:, 3] + lists[:, 4]
        kord_i = lists[:, 5].astype(jnp.int32)
        stream_base = (jnp.arange(NPAIR, dtype=jnp.int32) // N_EXPERTS) * NTOK
        src_rows = (tok_local + stream_base[:, None]).reshape(-1)
        g_cols = jnp.broadcast_to(g_sel[:, :, None], g_sel.shape + (LANES,))
        eo = _ffn(l, src_rows, hn, w_exp_gate, w_exp_up, w_exp_down, g_cols)
        cnt_cols = jnp.broadcast_to(cnt.reshape(NALL, 1), (NALL, LANES))
        dst_rows = (kord_i * TT + tok_local % TT).reshape(-1)
        xa, xb = _combine(offs, dst_rows, maxk, x1, mod[l], cnt_cols, g_final, eo, final=(l == DEPTH - 1))

    return xa.reshape(x_prompt.shape), xb.reshape(x_sample.shape), jnp.stack(states, axis=1)
```

```python
import functools

import jax
import jax.numpy as jnp
from jax import lax
from jax.experimental import pallas as pl
from jax.experimental.pallas import tpu as pltpu

F32 = jnp.float32
BF16 = jnp.bfloat16
MM = jnp.float32
HIGHEST = lax.Precision.HIGHEST

D = 1024
NTOK = 8192
NALL = 2 * NTOK
CTX_L = 256
LAT_L = 4096
GRID_W = 64
DEPTH = 2
D_POOL = 512
D_LRU = 512
D_IN = D_POOL + 2 * D_LRU
POOL_WINDOWS = (2, 4, 8, 16)
LANES = 128
N_EXPERTS = 16
CAP = 2 * NTOK // N_EXPERTS
D_EXPERT = 1024
LRU_C = 8.0
EPS = 1e-6

TM = 512
SEG = 4096
CH = 256
NCH = SEG // CH
CHS = CH + 8
TT = 256
NTT = NTOK // TT
VMEM_LIMIT = 56 * 1024 * 1024


def _cp(sem, vmem=VMEM_LIMIT):
    return pltpu.CompilerParams(dimension_semantics=sem, vmem_limit_bytes=vmem)


def _sigmoid(x):
    return 0.5 * jnp.tanh(0.5 * x) + 0.5


def _mod_row(i):
    n_ctx = NTOK // TM
    return jnp.where(i < n_ctx, 0, 1 + (i - n_ctx) // (LAT_L // TM))


def _mod_kernel(c_ref, w_ref, b_ref, o_ref):
    cv = c_ref[...]
    s = cv * jax.nn.sigmoid(cv)
    o_ref[0, 0] = jnp.dot(s, w_ref[0], precision=HIGHEST, preferred_element_type=F32) + b_ref[0, 0]


def _modulation(cvec, w_mod, b_mod):
    return pl.pallas_call(
        _mod_kernel,
        grid=(DEPTH, 6),
        in_specs=[
            pl.BlockSpec((8, D), lambda l, j: (0, 0)),
            pl.BlockSpec((1, D, D), lambda l, j: (l, 0, j)),
            pl.BlockSpec((1, 1, 1, D), lambda l, j: (l, j, 0, 0)),
        ],
        out_specs=pl.BlockSpec((1, 1, 8, D), lambda l, j: (l, j, 0, 0)),
        out_shape=jax.ShapeDtypeStruct((DEPTH, 6, 8, D), F32),
        compiler_params=_cp(("arbitrary", "arbitrary")),
        name="modulation",
    )(cvec, w_mod, b_mod.reshape(DEPTH, 6, 1, D))


def _rms(x, g):
    return x * lax.rsqrt(jnp.mean(x * x, axis=-1, keepdims=True) + EPS) * g


def _stream_specs(tile):
    n = NTOK // tile
    return [pl.BlockSpec((tile, D), lambda i, *_: (jnp.minimum(i, n - 1), 0)),
            pl.BlockSpec((tile, D), lambda i, *_: (jnp.maximum(i - n, 0), 0))]


def _stream_tile(xa_ref, xb_ref, tile):
    return jnp.where(pl.program_id(0) < NTOK // tile, xa_ref[...], xb_ref[...])


def _inproj_kernel(xa_ref, xb_ref, mod_ref, g_ref, w_ref, up_ref, ul_ref, ug_ref):
    x = _stream_tile(xa_ref, xb_ref, TM)
    hn = _rms(x, g_ref[...]) * (1.0 + mod_ref[0, 1:2, :]) + mod_ref[0, 0:1, :]
    proj = jnp.dot(hn.astype(MM), w_ref[...], preferred_element_type=F32)
    up_ref[...] = proj[:, :D_POOL]
    ul_ref[...] = proj[:, D_POOL:D_POOL + D_LRU]
    ug_ref[...] = proj[:, D_POOL + D_LRU:]


def _inproj(xa, xb, mod_l, g, w_in_bf):
    out = jax.ShapeDtypeStruct((NALL, D_POOL), F32)
    return pl.pallas_call(
        _inproj_kernel,
        grid=(NALL // TM,),
        in_specs=_stream_specs(TM) + [
            pl.BlockSpec((1, 8, D), lambda i: (_mod_row(i), 0, 0)),
            pl.BlockSpec((1, D), lambda i: (0, 0)),
            pl.BlockSpec((D, D_IN), lambda i: (0, 0)),
        ],
        out_specs=[pl.BlockSpec((TM, D_POOL), lambda i: (i, 0))] * 3,
        out_shape=[out, out, out],
        compiler_params=_cp(("arbitrary",)),
        name="inproj",
    )(xa, xb, mod_l, g.reshape(1, D), w_in_bf)


def _window_sum(x, h, pos, period):
    n = x.shape[0]
    acc = x
    for i in range(1, h):
        acc = acc + jnp.where(pos < period - i, pltpu.roll(x, n - i, axis=0), 0.0)
    for i in range(1, h + 1):
        acc = acc + jnp.where(pos >= i, pltpu.roll(x, i, axis=0), 0.0)
    return acc


def _window_count(pos, h, period):
    return (jnp.minimum(pos + h, period) - jnp.maximum(pos - h, 0)).astype(F32)


def _pool_kernel(u_ref, w_ref, s_ref, o_ref, pad_ref):
    seg = pl.program_id(0)
    row = lax.broadcasted_iota(jnp.int32, (CH, LANES), 0)
    halo = 8 * GRID_W

    def project(d, g, t0):
        y = jnp.dot(d.astype(MM), w_ref[g], preferred_element_type=F32)
        y = y * s_ref[:, g * LANES:(g + 1) * LANES]
        o_ref[pl.ds(t0, CH), g * LANES:(g + 1) * LANES] = y.astype(MM)

    @pl.when(seg < NTOK // SEG)
    def _context():
        def body(sb, c):
            t0 = pl.multiple_of(sb * CH, CH)
            for g, w in enumerate(POOL_WINDOWS):
                h = w // 2
                x = u_ref[pl.ds(t0, CH), g * LANES:(g + 1) * LANES]
                m = _window_sum(x, h, row, CTX_L) / _window_count(row, h, CTX_L)
                project(m - x, g, t0)
            return c
        lax.fori_loop(0, NCH, body, 0)

    @pl.when(seg >= NTOK // SEG)
    def _latent():
        zeros = jnp.zeros((halo, D_POOL), F32)
        pad_ref[0:halo, :] = zeros
        pad_ref[halo + SEG:halo + SEG + halo, :] = zeros
        col = row & (GRID_W - 1)

        def cols(sb, c):
            t0 = pl.multiple_of(sb * CH, CH)
            for g, w in enumerate(POOL_WINDOWS):
                h = w // 2
                x = u_ref[pl.ds(t0, CH), g * LANES:(g + 1) * LANES]
                m = _window_sum(x, h, col, GRID_W) / _window_count(col, h, GRID_W)
                pad_ref[pl.ds(halo + t0, CH), g * LANES:(g + 1) * LANES] = m
            return c
        lax.fori_loop(0, NCH, cols, 0)

        def rows(sb, c):
            t0 = pl.multiple_of(sb * CH, CH)
            r = jnp.right_shift(t0 + row, GRID_W.bit_length() - 1)
            for g, w in enumerate(POOL_WINDOWS):
                h = w // 2
                acc = None
                for i in range(-h, h):
                    start = pl.multiple_of(halo + t0 + i * GRID_W, GRID_W)
                    v = pad_ref[pl.ds(start, CH), g * LANES:(g + 1) * LANES]
                    acc = v if acc is None else acc + v
                m = acc / _window_count(r, h, GRID_W)
                x = u_ref[pl.ds(t0, CH), g * LANES:(g + 1) * LANES]
                project(m - x, g, t0)
            return c
        lax.fori_loop(0, NCH, rows, 0)


def _pool(u_pool, w_pool_bf, pool_scale):
    halo = 8 * GRID_W
    return pl.pallas_call(
        _pool_kernel,
        grid=(NALL // SEG,),
        in_specs=[
            pl.BlockSpec((SEG, D_POOL), lambda i: (i, 0)),
            pl.BlockSpec((len(POOL_WINDOWS), LANES, LANES), lambda i: (0, 0, 0)),
            pl.BlockSpec((1, D_POOL), lambda i: (0, 0)),
        ],
        out_specs=pl.BlockSpec((SEG, D_POOL), lambda i: (i, 0)),
        out_shape=jax.ShapeDtypeStruct((NALL, D_POOL), MM),
        scratch_shapes=[pltpu.VMEM((SEG + 2 * halo, D_POOL), F32)],
        compiler_params=_cp(("arbitrary",)),
        name="pool",
    )(u_pool, w_pool_bf, pool_scale.reshape(1, D_POOL))


def _lru_kernel(ul_ref, ug_ref, wc_ref, bc_ref, wg_ref, bg_ref, lam_ref, h0_ref,
                o_ref, st_ref, xc_ref, a_ref, b_ref, hin_ref):
    seg = pl.program_id(0)
    is_lat = seg >= NTOK // SEG
    last = jnp.where(is_lat, LAT_L - 1, CTX_L - 1)

    x = ul_ref[...]
    pos = lax.broadcasted_iota(jnp.int32, (SEG, LANES), 0) & last
    xc = bc_ref[...] + wc_ref[0:1, :] * jnp.where(pos >= 2, pltpu.roll(x, 2, axis=0), 0.0)
    xc = xc + wc_ref[1:2, :] * jnp.where(pos >= 1, pltpu.roll(x, 1, axis=0), 0.0)
    xc = xc + wc_ref[2:3, :] * x
    xc = xc + wc_ref[3:4, :] * jnp.where(pos < last, pltpu.roll(x, SEG - 1, axis=0), 0.0)
    xc_ref[...] = xc

    def gates(c, carry):
        t0 = pl.multiple_of(c * CH, CH)
        xcb = xc_ref[pl.ds(t0, CH), :]
        xb = xcb.astype(MM)
        hx = 0.5 * xcb
        for d in range(2):
            tr = jnp.tanh(jnp.dot(xb, wg_ref[0, 2 * d], preferred_element_type=F32) + bg_ref[2 * d:2 * d + 1, :])
            ti = jnp.tanh(jnp.dot(xb, wg_ref[0, 2 * d + 1], preferred_element_type=F32)
                          + bg_ref[2 * d + 1:2 * d + 2, :])
            half_c = (-0.5 * LRU_C) * jax.nn.softplus(-lam_ref[d:d + 1, :])
            log_a = half_c * tr + half_c
            ix = hx * ti + hx
            a = jnp.exp(log_a)
            m2 = jnp.maximum(-jnp.tanh(log_a) * (a * a + 1.0), 0.0)
            mult = jnp.where(m2 > 0.0, m2 * lax.rsqrt(m2), 0.0)
            s0 = pl.multiple_of(c * CHS, 8)
            a_ref[d, pl.ds(s0, CH), :] = a
            b_ref[d, pl.ds(s0, CH), :] = mult * ix
        return carry
    lax.fori_loop(0, NCH, gates, 0)

    def scan(j, carry):
        hf, pf, hb, pb = carry
        jb = CH - 1 - j
        sf = pl.ds(j, NCH, stride=CHS)
        sb = pl.ds(jb, NCH, stride=CHS)
        af = a_ref[0, sf, :]
        bf = b_ref[0, sf, :]
        ab = a_ref[1, sb, :]
        bb = b_ref[1, sb, :]
        hf = af * hf + bf
        pf = af * pf
        hb = ab * hb + bb
        pb = ab * pb
        b_ref[0, sf, :] = hf
        a_ref[0, sf, :] = pf
        b_ref[1, sb, :] = hb
        a_ref[1, sb, :] = pb
        return hf, pf, hb, pb
    z = jnp.zeros((NCH, LANES), F32)
    o = jnp.ones((NCH, LANES), F32)
    hf_end, pf_end, hb_beg, pb_beg = lax.fori_loop(0, CH, scan, (z, o, z, o))

    cur = h0_ref[0, 0:1, :]
    rows_f = []
    for c in range(NCH):
        rows_f.append(cur)
        cur = pf_end[c:c + 1, :] * cur + hf_end[c:c + 1, :]
    cur = h0_ref[0, 1:2, :]
    rows_b = [None] * NCH
    for c in range(NCH - 1, -1, -1):
        rows_b[c] = cur
        cur = pb_beg[c:c + 1, :] * cur + hb_beg[c:c + 1, :]
    hin_f = jnp.where(is_lat, jnp.concatenate(rows_f, axis=0), 0.0)
    hin_b = jnp.where(is_lat, jnp.concatenate(rows_b, axis=0), 0.0)
    hin_ref[0] = hin_f
    hin_ref[1] = hin_b
    st_ref[0, 0] = pf_end * hin_f + hf_end
    st_ref[0, 1] = pb_beg * hin_b + hb_beg

    def finish(c, carry):
        t0 = pl.multiple_of(c * CH, CH)
        rows = pl.ds(pl.multiple_of(c * CHS, 8), CH)
        hf = a_ref[0, rows, :] * hin_ref[0, pl.ds(c, 1), :] + b_ref[0, rows, :]
        hb = a_ref[1, rows, :] * hin_ref[1, pl.ds(c, 1), :] + b_ref[1, rows, :]
        y = jax.nn.gelu(ug_ref[pl.ds(t0, CH), :]) * (hf + hb)
        o_ref[pl.ds(t0, CH), :] = y.astype(MM)
        return carry
    lax.fori_loop(0, NCH, finish, 0)


def _lru(u_lru, u_gate, w_conv, b_conv, wg_bd, bg, lam, h0):
    nseg = NALL // SEG
    ncb = D_LRU // LANES
    return pl.pallas_call(
        _lru_kernel,
        grid=(nseg, ncb),
        in_specs=[
            pl.BlockSpec((SEG, LANES), lambda s, c: (s, c)),
            pl.BlockSpec((SEG, LANES), lambda s, c: (s, c)),
            pl.BlockSpec((4, LANES), lambda s, c: (0, c)),
            pl.BlockSpec((1, LANES), lambda s, c: (0, c)),
            pl.BlockSpec((1, 4, LANES, LANES), lambda s, c: (c, 0, 0, 0)),
            pl.BlockSpec((4, LANES), lambda s, c: (0, c)),
            pl.BlockSpec((2, LANES), lambda s, c: (0, c)),
            pl.BlockSpec((1, 2, LANES), lambda s, c: (s, 0, c)),
        ],
        out_specs=[
            pl.BlockSpec((SEG, LANES), lambda s, c: (s, c)),
            pl.BlockSpec((1, 2, NCH, LANES), lambda s, c: (s, 0, 0, c)),
        ],
        out_shape=[
            jax.ShapeDtypeStruct((NALL, D_LRU), MM),
            jax.ShapeDtypeStruct((nseg, 2, NCH, D_LRU), F32),
        ],
        scratch_shapes=[
            pltpu.VMEM((SEG, LANES), F32),
            pltpu.VMEM((2, NCH * CHS, LANES), F32),
            pltpu.VMEM((2, NCH * CHS, LANES), F32),
            pltpu.VMEM((2, NCH, LANES), F32),
        ],
        compiler_params=_cp(("arbitrary", "arbitrary")),
        name="lru",
    )(u_lru, u_gate, w_conv, b_conv.reshape(1, D_LRU), wg_bd, bg, lam, h0)


RT = D // LANES


def _store_row_tiles(ref, start, n, v):
    for c in range(RT):
        ref[pl.ds(start * RT + c, n, stride=RT), :] = v[:, c * LANES:(c + 1) * LANES]


def _load_row_tiles(ref, start, n):
    return jnp.concatenate([ref[pl.ds(start * RT + c, n, stride=RT), :] for c in range(RT)], axis=1)


def _outproj_kernel(xa_ref, xb_ref, yp_ref, yl_ref, mod_ref, g_ref, wo_ref, wr_ref, x1_ref, hn_ref, aff_ref):
    mix = jnp.dot(yp_ref[...], wo_ref[0:D_POOL, :], preferred_element_type=F32)
    mix = mix + jnp.dot(yl_ref[...], wo_ref[D_POOL:, :], preferred_element_type=F32)
    x1 = _stream_tile(xa_ref, xb_ref, TM) + mod_ref[0, 2:3, :] * mix
    x1_ref[...] = x1
    hn = _rms(x1, g_ref[...]) * (1.0 + mod_ref[0, 4:5, :]) + mod_ref[0, 3:4, :]
    _store_row_tiles(hn_ref, 0, TM, hn)
    hn_hi = hn.astype(BF16)
    hn_lo = (hn - hn_hi.astype(F32)).astype(BF16)
    nt = (((1,), (1,)), ((), ()))
    logits = (lax.dot_general(wr_ref[0], hn_hi, nt, preferred_element_type=F32)
              + lax.dot_general(wr_ref[0], hn_lo, nt, preferred_element_type=F32)
              + lax.dot_general(wr_ref[1], hn_hi, nt, preferred_element_type=F32))
    e = jnp.exp(logits - jnp.max(logits, axis=0, keepdims=True))
    aff_ref[...] = e / jnp.sum(e, axis=0, keepdims=True)


def _outproj(xa, xb, y_pool, y_lru, mod_l, g, w_out_bf, w_router_t):
    return pl.pallas_call(
        _outproj_kernel,
        grid=(NALL // TM,),
        in_specs=_stream_specs(TM) + [
            pl.BlockSpec((TM, D_POOL), lambda i: (i, 0)),
            pl.BlockSpec((TM, D_LRU), lambda i: (i, 0)),
            pl.BlockSpec((1, 8, D), lambda i: (_mod_row(i), 0, 0)),
            pl.BlockSpec((1, D), lambda i: (0, 0)),
            pl.BlockSpec((D, D), lambda i: (0, 0)),
            pl.BlockSpec((2, N_EXPERTS, D), lambda i: (0, 0, 0)),
        ],
        out_specs=[
            pl.BlockSpec((TM, D), lambda i: (i, 0)),
            pl.BlockSpec((TM * RT, LANES), lambda i: (i, 0)),
            pl.BlockSpec((N_EXPERTS, TM), lambda i: (0, i)),
        ],
        out_shape=[
            jax.ShapeDtypeStruct((NALL, D), F32),
            jax.ShapeDtypeStruct((NALL * RT, LANES), F32),
            jax.ShapeDtypeStruct((N_EXPERTS, NALL), F32),
        ],
        compiler_params=_cp(("arbitrary",)),
        name="outproj",
    )(xa, xb, y_pool, y_lru, mod_l, g.reshape(1, D), w_out_bf, w_router_t)


def _lane_prefix(m, tri):
    parts = []
    bounds = []
    run = jnp.zeros((m.shape[0], 1), F32)
    for k in range(NTOK // LANES):
        if (k * LANES) % TT == 0:
            bounds.append(run)
        p = jnp.dot(m[:, k * LANES:(k + 1) * LANES].astype(BF16), tri, preferred_element_type=F32)
        parts.append(p + run)
        run = run + p[:, LANES - 1:LANES]
    bounds.append(run)
    return jnp.concatenate(parts, axis=1), bounds


def _plan_kernel(aff_ref, slot_ref, kord_ref, cnt_ref, off_ref, maxk_ref):
    aff = aff_ref[...]
    bits = pltpu.bitcast(aff, jnp.int32)

    def count_ge(v):
        return jnp.sum(jnp.where(bits >= v, 1.0, 0.0), axis=1, keepdims=True)

    def bisect(_, c):
        lo, hi = c
        mid = lo + jnp.right_shift(hi - lo + 1, 1)
        ok = count_ge(mid) >= float(CAP)
        return jnp.where(ok, mid, lo), jnp.where(ok, hi, mid - 1)
    lo0 = jnp.zeros((N_EXPERTS, 1), jnp.int32)
    hi0 = jnp.full((N_EXPERTS, 1), 0x7F7FFFFF, jnp.int32)
    thr, _ = lax.fori_loop(0, 31, bisect, (lo0, hi0))

    ri = lax.broadcasted_iota(jnp.int32, (LANES, LANES), 0)
    ci = lax.broadcasted_iota(jnp.int32, (LANES, LANES), 1)
    tri = jnp.where(ri <= ci, 1.0, 0.0).astype(BF16)

    gt = bits > thr
    eq = bits == thr
    need = float(CAP) - jnp.sum(jnp.where(gt, 1.0, 0.0), axis=1, keepdims=True)
    eq_rank, _ = _lane_prefix(jnp.where(eq, 1.0, 0.0), tri)
    sel = jnp.where(gt, 1.0, jnp.where(eq & (eq_rank <= need), 1.0, 0.0))
    rank, bounds = _lane_prefix(sel, tri)
    slot_ref[0] = jnp.where(sel > 0.0, rank - 1.0, -1.0)

    er = lax.broadcasted_iota(jnp.int32, (N_EXPERTS, N_EXPERTS), 0)
    ec = lax.broadcasted_iota(jnp.int32, (N_EXPERTS, N_EXPERTS), 1)
    below = jnp.where(ec < er, 1.0, 0.0).astype(BF16)
    kord_ref[0] = jnp.dot(below, sel.astype(BF16), preferred_element_type=F32)
    cnt = jnp.sum(sel, axis=0, keepdims=True)
    cnt_ref[0] = cnt

    lane = lax.broadcasted_iota(jnp.int32, (N_EXPERTS, LANES), 1)
    offs = jnp.zeros((N_EXPERTS, LANES), F32)
    for b, col in enumerate(bounds):
        offs = jnp.where(lane == b, col, offs)
    off_ref[0] = offs.astype(jnp.int32)

    lane1 = lax.broadcasted_iota(jnp.int32, (1, LANES), 1)
    mk = jnp.zeros((1, LANES), F32)
    for b in range(NTT):
        mk = jnp.where(lane1 == b, jnp.max(cnt[:, b * TT:(b + 1) * TT], axis=1, keepdims=True), mk)
    maxk_ref[0] = mk.astype(jnp.int32)


def _plan(aff_t):
    ns = NALL // NTOK
    return pl.pallas_call(
        _plan_kernel,
        grid=(ns,),
        in_specs=[pl.BlockSpec((N_EXPERTS, NTOK), lambda s: (0, s))],
        out_specs=[
            pl.BlockSpec((1, N_EXPERTS, NTOK), lambda s: (s, 0, 0)),
            pl.BlockSpec((1, N_EXPERTS, NTOK), lambda s: (s, 0, 0)),
            pl.BlockSpec((1, 1, NTOK), lambda s: (s, 0, 0)),
            pl.BlockSpec((1, N_EXPERTS, LANES), lambda s: (s, 0, 0)),
            pl.BlockSpec((1, 1, LANES), lambda s: (s, 0, 0)),
        ],
        out_shape=[
            jax.ShapeDtypeStruct((ns, N_EXPERTS, NTOK), F32),
            jax.ShapeDtypeStruct((ns, N_EXPERTS, NTOK), F32),
            jax.ShapeDtypeStruct((ns, 1, NTOK), F32),
            jax.ShapeDtypeStruct((ns, N_EXPERTS, LANES), jnp.int32),
            jax.ShapeDtypeStruct((ns, 1, LANES), jnp.int32),
        ],
        compiler_params=_cp(("arbitrary",)),
        name="plan",
    )(aff_t)


COMPACT_UNROLL = 16


def _compact_kernel(off_ref, slot_ref, aff_ref, kord_ref, o_ref):
    p = pl.program_id(0)
    nj = CAP // TT
    sub = lax.broadcasted_iota(jnp.int32, (TT, TT), 0).astype(F32)
    lane = lax.broadcasted_iota(jnp.int32, (1, TT), 1).astype(F32)
    lane8 = lax.broadcasted_iota(jnp.int32, (8, TT), 1)

    def tile(b, accs):
        lo = off_ref[p * LANES + b]
        t0 = pl.multiple_of(b * TT, TT)
        srow = slot_ref[0, :, pl.ds(t0, TT)]
        g = aff_ref[0, :, pl.ds(t0, TT)]
        g_hi = g.astype(BF16)
        r1 = g - g_hi.astype(F32)
        g_mid = r1.astype(BF16)
        g_lo = (r1 - g_mid.astype(F32)).astype(BF16)
        tile_id = jnp.full((1, TT), b, jnp.int32).astype(F32)
        data = jnp.concatenate([
            tile_id.astype(BF16), lane.astype(BF16), g_hi, g_mid, g_lo,
            kord_ref[0, :, pl.ds(t0, TT)].astype(BF16),
            jnp.zeros((2, TT), BF16)], axis=0)
        onehot = jnp.where(srow - lo.astype(F32) == sub, 1.0, 0.0).astype(BF16)
        res = lax.dot_general(data, onehot, (((1,), (1,)), ((), ())), preferred_element_type=F32)
        shift = lo & (TT - 1)
        j0 = lo // TT
        rolled = pltpu.roll(res, shift, axis=1)
        first = lane8 >= shift
        out = []
        for j in range(nj):
            mine = jnp.logical_or(jnp.logical_and(first, j == j0),
                                  jnp.logical_and(jnp.logical_not(first), j == j0 + 1))
            out.append(accs[j] + jnp.where(mine, rolled, 0.0))
        return tuple(out)

    def tiles(i, accs):
        for u in range(COMPACT_UNROLL):
            accs = tile(i * COMPACT_UNROLL + u, accs)
        return accs
    zero = jnp.zeros((8, TT), F32)
    accs = lax.fori_loop(0, NTT // COMPACT_UNROLL, tiles, (zero,) * nj)
    for j in range(nj):
        o_ref[0, :, j * TT:(j + 1) * TT] = accs[j]


def _compact(offs, slot, aff_rows, kord):
    npair = (NALL // NTOK) * N_EXPERTS
    spec = pl.BlockSpec((1, 1, NTOK), lambda p, off: (p, 0, 0))
    return pl.pallas_call(
        _compact_kernel,
        grid_spec=pltpu.PrefetchScalarGridSpec(
            num_scalar_prefetch=1,
            grid=(npair,),
            in_specs=[spec, spec, spec],
            out_specs=pl.BlockSpec((1, 8, CAP), lambda p, off: (p, 0, 0)),
        ),
        out_shape=jax.ShapeDtypeStruct((npair, 8, CAP), F32),
        compiler_params=_cp(("arbitrary",)),
        name="compact",
    )(offs.reshape(-1), slot.reshape(npair, 1, NTOK), aff_rows.reshape(npair, 1, NTOK),
      kord.reshape(npair, 1, NTOK))


FSPLIT = 2
NPAIR = (NALL // NTOK) * N_EXPERTS
GATHER_UNROLL = 8


def _ffn_kernel(idx_ref, hn_ref, wg_ref, wu_ref, wd_ref, g_ref, o_ref, xs_a, xs_b, acc_ref, sem):
    p = pl.program_id(0)
    f = pl.program_id(1)
    share = CAP // FSPLIT

    def start_row(pair, dst_ref, dst_sem, s):
        src = pl.multiple_of(idx_ref[pair * CAP + s] * RT, RT)
        pltpu.make_async_copy(hn_ref.at[pl.ds(src, RT)], dst_ref.at[pl.ds(s * RT, RT)], dst_sem).start()

    def wait_rows(ref, ref_sem):
        pltpu.make_async_copy(hn_ref.at[pl.ds(0, CAP * RT)], ref, ref_sem).wait()

    @pl.when(jnp.logical_and(p == 0, f == 0))
    def _():
        acc_ref[...] = jnp.zeros_like(acc_ref)

        def issue(i, c):
            for u in range(GATHER_UNROLL):
                start_row(0, xs_a, sem.at[0], i * GATHER_UNROLL + u)
            return c
        lax.fori_loop(0, CAP // GATHER_UNROLL, issue, 0)

    def step(cur, cur_sem, nxt, nxt_sem):
        @pl.when(f == 0)
        def _():
            wait_rows(cur, cur_sem)

        nxt_pair = jnp.minimum(p + 1, NPAIR - 1)
        for s in range(share):
            start_row(nxt_pair, nxt, nxt_sem, f * share + s)

        x = _load_row_tiles(cur, 0, CAP)
        hg = jnp.dot(x, wg_ref[0, 0], preferred_element_type=F32)
        hu = jnp.dot(x, wu_ref[0, 0], preferred_element_type=F32)
        hid = hg * _sigmoid(hg) * hu
        o = jnp.dot(hid, wd_ref[0, 0], preferred_element_type=F32)

        acc = jnp.where(f > 0, acc_ref[...], 0.0) + o
        acc_ref[...] = acc
        _store_row_tiles(o_ref, 0, CAP, acc * g_ref[0][:, 0:1])

        @pl.when(jnp.logical_and(p == NPAIR - 1, f == FSPLIT - 1))
        def _():
            wait_rows(nxt, nxt_sem)

    @pl.when(p % 2 == 0)
    def _():
        step(xs_a, sem.at[0], xs_b, sem.at[1])

    @pl.when(p % 2 == 1)
    def _():
        step(xs_b, sem.at[1], xs_a, sem.at[0])


def _ffn(l, idx_rows, hn, w_gate, w_up, w_down, g_cols):
    fw = D_EXPERT // FSPLIT
    return pl.pallas_call(
        _ffn_kernel,
        grid_spec=pltpu.PrefetchScalarGridSpec(
            num_scalar_prefetch=1,
            grid=(NPAIR, FSPLIT),
            in_specs=[
                pl.BlockSpec(memory_space=pl.ANY),
                pl.BlockSpec((1, 1, D, fw), lambda p, f, idx: (l, p % N_EXPERTS, 0, f)),
                pl.BlockSpec((1, 1, D, fw), lambda p, f, idx: (l, p % N_EXPERTS, 0, f)),
                pl.BlockSpec((1, 1, fw, D), lambda p, f, idx: (l, p % N_EXPERTS, f, 0)),
                pl.BlockSpec((1, CAP, LANES), lambda p, f, idx: (p, 0, 0)),
            ],
            out_specs=pl.BlockSpec((CAP * RT, LANES), lambda p, f, idx: (p, 0)),
            scratch_shapes=[
                pltpu.VMEM((CAP * RT, LANES), F32),
                pltpu.VMEM((CAP * RT, LANES), F32),
                pltpu.VMEM((CAP, D), F32),
                pltpu.SemaphoreType.DMA((2,)),
            ],
        ),
        out_shape=jax.ShapeDtypeStruct((NPAIR * CAP * RT, LANES), F32),
        compiler_params=_cp(("arbitrary", "arbitrary")),
        name="ffn",
    )(idx_rows, hn, w_gate, w_up, w_down, g_cols)


COMBINE_UNROLL = 8
STAGE_ROWS = N_EXPERTS * TT


def _combine_kernel(off_ref, dst_ref, maxk_ref, x1_ref, mod_ref, cnt_ref, gf_ref, eo_ref,
                    octx_ref, o_ref, stage_ref, sem, *, final):
    i = pl.program_id(0)
    ntile = pl.num_programs(0)
    buf = i % 2

    def fetch_tile(tile, slot):
        s = tile // NTT
        b = tile % NTT
        slot_base = slot * STAGE_ROWS

        def fetch(r):
            dst = pl.multiple_of((slot_base + dst_ref[r]) * RT, RT)
            pltpu.make_async_copy(eo_ref.at[pl.ds(pl.multiple_of(r * RT, RT), RT)], stage_ref.at[pl.ds(dst, RT)],
                                  sem.at[slot]).start()

        for e in range(N_EXPERTS):
            pe = s * N_EXPERTS + e
            lo = off_ref[pe * LANES + b]
            n = off_ref[pe * LANES + b + 1] - lo
            base = pe * CAP + lo
            nq = lax.shift_right_logical(n, COMBINE_UNROLL.bit_length() - 1)

            def issue_many(q, c, base=base):
                for u in range(COMBINE_UNROLL):
                    fetch(base + q * COMBINE_UNROLL + u)
                return c
            lax.fori_loop(0, nq, issue_many, 0)

            def issue_one(r, c):
                fetch(r)
                return c
            lax.fori_loop(base + nq * COMBINE_UNROLL, base + n, issue_one, 0)

    def rows_of_tile(tile):
        s = tile // NTT
        b = tile % NTT
        total = 0
        for e in range(N_EXPERTS):
            pe = s * N_EXPERTS + e
            total = total + off_ref[pe * LANES + b + 1] - off_ref[pe * LANES + b]
        return total

    @pl.when(i == 0)
    def _():
        fetch_tile(i, buf)

    @pl.when(i + 1 < ntile)
    def _():
        fetch_tile(i + 1, 1 - buf)

    total = rows_of_tile(i)

    @pl.when(total > 0)
    def _():
        pltpu.make_async_copy(eo_ref.at[pl.ds(0, total * RT)], stage_ref.at[pl.ds(0, total * RT)],
                              sem.at[buf]).wait()

    cnt = cnt_ref[:, 0:1]
    o_ref[...] = jnp.zeros_like(o_ref)

    def add(k, c):
        rows = _load_row_tiles(stage_ref, pl.multiple_of(buf * STAGE_ROWS + k * TT, TT), TT)
        o_ref[...] += jnp.where(cnt > k.astype(F32), rows, 0.0)
        return c
    lax.fori_loop(0, maxk_ref[(i // NTT) * LANES + i % NTT], add, 0)
    x2 = x1_ref[...] + mod_ref[0, 5:6, :] * o_ref[...]
    if final:
        x2 = _rms(x2, gf_ref[...])
    o_ref[...] = x2

    @pl.when(i < NTT)
    def _():
        octx_ref[...] = x2


def _combine(offs, dst_rows, maxk, x1, mod_l, cnt_cols, g_final, eo, final):
    def mrow(i):
        return jnp.where(i < NTT, 0, 1 + (i - NTT) // (LAT_L // TT))
    return pl.pallas_call(
        functools.partial(_combine_kernel, final=final),
        grid_spec=pltpu.PrefetchScalarGridSpec(
            num_scalar_prefetch=3,
            grid=(NALL // TT,),
            in_specs=[
                pl.BlockSpec((TT, D), lambda i, *_: (i, 0)),
                pl.BlockSpec((1, 8, D), lambda i, *_: (mrow(i), 0, 0)),
                pl.BlockSpec((TT, LANES), lambda i, *_: (i, 0)),
                pl.BlockSpec((1, D), lambda i, *_: (0, 0)),
                pl.BlockSpec(memory_space=pl.ANY),
            ],
            out_specs=_stream_specs(TT),
            scratch_shapes=[pltpu.VMEM((2 * STAGE_ROWS * RT, LANES), F32), pltpu.SemaphoreType.DMA((2,))],
        ),
        out_shape=[jax.ShapeDtypeStruct((NTOK, D), F32)] * 2,
        compiler_params=_cp(("arbitrary",)),
        name="combine",
    )(offs.reshape(-1), dst_rows, maxk.reshape(-1), x1, mod_l, cnt_cols, g_final.reshape(1, D), eo)


def _gate_blocks(w_a, w_x):
    hd = w_a.shape[-1]
    per = LANES // hd
    ncb = D_LRU // LANES
    mats = jnp.stack([w_a[0], w_x[0], w_a[1], w_x[1]], axis=0)
    mats = mats.reshape(4, ncb, per, hd, hd)
    out = jnp.zeros((ncb, 4, LANES, LANES), F32)
    for q in range(per):
        out = out.at[:, :, q * hd:(q + 1) * hd, q * hd:(q + 1) * hd].set(mats[:, :, q].transpose(1, 0, 2, 3))
    return out.astype(MM)


def kernel(x_prompt, x_sample, state_rglru, c, c_ctx, w_mod, b_mod, g_norm_mix, g_norm_ffn, w_in, w_pool,
           pool_scale, w_conv, b_conv, w_gate_a, b_gate_a, w_gate_x, b_gate_x, lru_lambda, w_out, w_router,
           w_exp_gate, w_exp_up, w_exp_down, g_final):
    nb_ctx = x_prompt.shape[0]
    nb_lat = x_sample.shape[0]
    xa, xb = x_prompt.reshape(NTOK, D), x_sample.reshape(NTOK, D)

    cvec = jnp.zeros((8, D), F32).at[0].set(c_ctx).at[1:1 + nb_lat].set(c)
    mod = _modulation(cvec, w_mod, b_mod)
    mod = jnp.pad(mod.transpose(0, 2, 1, 3), ((0, 0), (0, 0), (0, 2), (0, 0)))

    nseg = NALL // SEG
    states = []
    for l in range(DEPTH):
        u_pool, u_lru, u_gate = _inproj(xa, xb, mod[l], g_norm_mix[l], w_in[l].astype(MM))
        y_pool = _pool(u_pool, w_pool[l].astype(MM), pool_scale[l])
        h0 = jnp.zeros((nseg, 2, D_LRU), F32).at[NTOK // SEG:].set(state_rglru[:, l])
        bg = jnp.stack([b_gate_a[l, 0], b_gate_x[l, 0], b_gate_a[l, 1], b_gate_x[l, 1]], axis=0).reshape(4, D_LRU)
        y_lru, st = _lru(u_lru, u_gate, w_conv[l], b_conv[l], 0.5 * _gate_blocks(w_gate_a[l], w_gate_x[l]),
                         0.5 * bg, lru_lambda[l], h0)
        states.append(st[:NTOK // SEG].transpose(0, 2, 1, 3).reshape(nb_ctx, 2, D_LRU))

        wr = w_router[l].T
        wr_hi = wr.astype(BF16)
        wr_split = jnp.stack([wr_hi, (wr - wr_hi.astype(F32)).astype(BF16)], axis=0)
        x1, hn, aff_t = _outproj(xa, xb, y_pool, y_lru, mod[l], g_norm_ffn[l], w_out[l].astype(MM), wr_split)
        slot, kord, cnt, offs, maxk = _plan(aff_t)
        aff_rows = aff_t.reshape(N_EXPERTS, NALL // NTOK, NTOK).transpose(1, 0, 2)
        lists = _compact(offs, slot, aff_rows, kord)
        tok_local = (lists[:, 0] * TT + lists[:, 1]).astype(jnp.int32)
        g_sel = lists[:, 2] + lists[:, 3] + lists[:, 4]
        kord_i = lists[:, 5].astype(jnp.int32)
        stream_base = (jnp.arange(NPAIR, dtype=jnp.int32) // N_EXPERTS) * NTOK
        src_rows = (tok_local + stream_base[:, None]).reshape(-1)
        g_cols = jnp.broadcast_to(g_sel[:, :, None], g_sel.shape + (LANES,))
        eo = _ffn(l, src_rows, hn, w_exp_gate, w_exp_up, w_exp_down, g_cols)
        cnt_cols = jnp.broadcast_to(cnt.reshape(NALL, 1), (NALL, LANES))
        dst_rows = (kord_i * TT + tok_local % TT).reshape(-1)
        xa, xb = _combine(offs, dst_rows, maxk, x1, mod[l], cnt_cols, g_final, eo, final=(l == DEPTH - 1))

    return xa.reshape(x_prompt.shape), xb.reshape(x_sample.shape), jnp.stack(states, axis=1)
```

```python
import functools

import jax
import jax.numpy as jnp
from jax import lax
from jax.experimental import pallas as pl
from jax.experimental.pallas import tpu as pltpu

F32 = jnp.float32
BF16 = jnp.bfloat16
MM = jnp.float32
HIGHEST = lax.Precision.HIGHEST

D = 1024
NTOK = 8192
NALL = 2 * NTOK
CTX_L = 256
LAT_L = 4096
GRID_W = 64
DEPTH = 2
D_POOL = 512
D_LRU = 512
D_IN = D_POOL + 2 * D_LRU
POOL_WINDOWS = (2, 4, 8, 16)
LANES = 128
N_EXPERTS = 16
CAP = 2 * NTOK // N_EXPERTS
D_EXPERT = 1024
LRU_C = 8.0
EPS = 1e-6

TM = 512
SEG = 4096
CH = 256
NCH = SEG // CH
CHS = CH + 8
TT = 256
NTT = NTOK // TT
VMEM_LIMIT = 56 * 1024 * 1024


def _cp(sem, vmem=VMEM_LIMIT):
    return pltpu.CompilerParams(dimension_semantics=sem, vmem_limit_bytes=vmem)


def _sigmoid(x):
    return 0.5 * jnp.tanh(0.5 * x) + 0.5


def _mod_row(i):
    n_ctx = NTOK // TM
    return jnp.where(i < n_ctx, 0, 1 + (i - n_ctx) // (LAT_L // TM))


def _mod_kernel(c_ref, w_ref, b_ref, o_ref):
    cv = c_ref[...]
    s = cv * jax.nn.sigmoid(cv)
    o_ref[0, 0] = jnp.dot(s, w_ref[0], precision=HIGHEST, preferred_element_type=F32) + b_ref[0, 0]


def _modulation(cvec, w_mod, b_mod):
    return pl.pallas_call(
        _mod_kernel,
        grid=(DEPTH, 6),
        in_specs=[
            pl.BlockSpec((8, D), lambda l, j: (0, 0)),
            pl.BlockSpec((1, D, D), lambda l, j: (l, 0, j)),
            pl.BlockSpec((1, 1, 1, D), lambda l, j: (l, j, 0, 0)),
        ],
        out_specs=pl.BlockSpec((1, 1, 8, D), lambda l, j: (l, j, 0, 0)),
        out_shape=jax.ShapeDtypeStruct((DEPTH, 6, 8, D), F32),
        compiler_params=_cp(("arbitrary", "arbitrary")),
        name="modulation",
    )(cvec, w_mod, b_mod.reshape(DEPTH, 6, 1, D))


def _rms(x, g):
    return x * lax.rsqrt(jnp.mean(x * x, axis=-1, keepdims=True) + EPS) * g


def _stream_specs(tile):
    n = NTOK // tile
    return [pl.BlockSpec((tile, D), lambda i, *_: (jnp.minimum(i, n - 1), 0)),
            pl.BlockSpec((tile, D), lambda i, *_: (jnp.maximum(i - n, 0), 0))]


def _stream_tile(xa_ref, xb_ref, tile):
    return jnp.where(pl.program_id(0) < NTOK // tile, xa_ref[...], xb_ref[...])


def _inproj_kernel(xa_ref, xb_ref, mod_ref, g_ref, w_ref, up_ref, ul_ref, ug_ref):
    x = _stream_tile(xa_ref, xb_ref, TM)
    hn = _rms(x, g_ref[...]) * (1.0 + mod_ref[0, 1:2, :]) + mod_ref[0, 0:1, :]
    proj = jnp.dot(hn.astype(MM), w_ref[...], preferred_element_type=F32)
    up_ref[...] = proj[:, :D_POOL]
    ul_ref[...] = proj[:, D_POOL:D_POOL + D_LRU]
    ug_ref[...] = proj[:, D_POOL + D_LRU:]


def _inproj(xa, xb, mod_l, g, w_in_bf):
    out = jax.ShapeDtypeStruct((NALL, D_POOL), F32)
    return pl.pallas_call(
        _inproj_kernel,
        grid=(NALL // TM,),
        in_specs=_stream_specs(TM) + [
            pl.BlockSpec((1, 8, D), lambda i: (_mod_row(i), 0, 0)),
            pl.BlockSpec((1, D), lambda i: (0, 0)),
            pl.BlockSpec((D, D_IN), lambda i: (0, 0)),
        ],
        out_specs=[pl.BlockSpec((TM, D_POOL), lambda i: (i, 0))] * 3,
        out_shape=[out, out, out],
        compiler_params=_cp(("arbitrary",)),
        name="inproj",
    )(xa, xb, mod_l, g.reshape(1, D), w_in_bf)


def _window_sum(x, h, pos, period):
    n = x.shape[0]
    acc = x
    for i in range(1, h):
        acc = acc + jnp.where(pos < period - i, pltpu.roll(x, n - i, axis=0), 0.0)
    for i in range(1, h + 1):
        acc = acc + jnp.where(pos >= i, pltpu.roll(x, i, axis=0), 0.0)
    return acc


def _window_count(pos, h, period):
    return (jnp.minimum(pos + h, period) - jnp.maximum(pos - h, 0)).astype(F32)


def _pool_kernel(u_ref, w_ref, s_ref, o_ref, pad_ref):
    seg = pl.program_id(0)
    row = lax.broadcasted_iota(jnp.int32, (CH, LANES), 0)
    halo = 8 * GRID_W

    def project(d, g, t0):
        y = jnp.dot(d.astype(MM), w_ref[g], preferred_element_type=F32)
        y = y * s_ref[:, g * LANES:(g + 1) * LANES]
        o_ref[pl.ds(t0, CH), g * LANES:(g + 1) * LANES] = y.astype(MM)

    @pl.when(seg < NTOK // SEG)
    def _context():
        def body(sb, c):
            t0 = pl.multiple_of(sb * CH, CH)
            for g, w in enumerate(POOL_WINDOWS):
                h = w // 2
                x = u_ref[pl.ds(t0, CH), g * LANES:(g + 1) * LANES]
                m = _window_sum(x, h, row, CTX_L) / _window_count(row, h, CTX_L)
                project(m - x, g, t0)
            return c
        lax.fori_loop(0, NCH, body, 0)

    @pl.when(seg >= NTOK // SEG)
    def _latent():
        zeros = jnp.zeros((halo, D_POOL), F32)
        pad_ref[0:halo, :] = zeros
        pad_ref[halo + SEG:halo + SEG + halo, :] = zeros
        col = row & (GRID_W - 1)

        def cols(sb, c):
            t0 = pl.multiple_of(sb * CH, CH)
            for g, w in enumerate(POOL_WINDOWS):
                h = w // 2
                x = u_ref[pl.ds(t0, CH), g * LANES:(g + 1) * LANES]
                m = _window_sum(x, h, col, GRID_W) / _window_count(col, h, GRID_W)
                pad_ref[pl.ds(halo + t0, CH), g * LANES:(g + 1) * LANES] = m
            return c
        lax.fori_loop(0, NCH, cols, 0)

        def rows(sb, c):
            t0 = pl.multiple_of(sb * CH, CH)
            r = jnp.right_shift(t0 + row, GRID_W.bit_length() - 1)
            for g, w in enumerate(POOL_WINDOWS):
                h = w // 2
                acc = None
                for i in range(-h, h):
                    start = pl.multiple_of(halo + t0 + i * GRID_W, GRID_W)
                    v = pad_ref[pl.ds(start, CH), g * LANES:(g + 1) * LANES]
                    acc = v if acc is None else acc + v
                m = acc / _window_count(r, h, GRID_W)
                x = u_ref[pl.ds(t0, CH), g * LANES:(g + 1) * LANES]
                project(m - x, g, t0)
            return c
        lax.fori_loop(0, NCH, rows, 0)


def _pool(u_pool, w_pool_bf, pool_scale):
    halo = 8 * GRID_W
    return pl.pallas_call(
        _pool_kernel,
        grid=(NALL // SEG,),
        in_specs=[
            pl.BlockSpec((SEG, D_POOL), lambda i: (i, 0)),
            pl.BlockSpec((len(POOL_WINDOWS), LANES, LANES), lambda i: (0, 0, 0)),
            pl.BlockSpec((1, D_POOL), lambda i: (0, 0)),
        ],
        out_specs=pl.BlockSpec((SEG, D_POOL), lambda i: (i, 0)),
        out_shape=jax.ShapeDtypeStruct((NALL, D_POOL), MM),
        scratch_shapes=[pltpu.VMEM((SEG + 2 * halo, D_POOL), F32)],
        compiler_params=_cp(("arbitrary",)),
        name="pool",
    )(u_pool, w_pool_bf, pool_scale.reshape(1, D_POOL))


def _lru_kernel(ul_ref, ug_ref, wc_ref, bc_ref, wg_ref, bg_ref, lam_ref, h0_ref,
                o_ref, st_ref, xc_ref, a_ref, b_ref, hin_ref):
    seg = pl.program_id(0)
    is_lat = seg >= NTOK // SEG
    last = jnp.where(is_lat, LAT_L - 1, CTX_L - 1)

    x = ul_ref[...]
    pos = lax.broadcasted_iota(jnp.int32, (SEG, LANES), 0) & last
    xc = bc_ref[...] + wc_ref[0:1, :] * jnp.where(pos >= 2, pltpu.roll(x, 2, axis=0), 0.0)
    xc = xc + wc_ref[1:2, :] * jnp.where(pos >= 1, pltpu.roll(x, 1, axis=0), 0.0)
    xc = xc + wc_ref[2:3, :] * x
    xc = xc + wc_ref[3:4, :] * jnp.where(pos < last, pltpu.roll(x, SEG - 1, axis=0), 0.0)
    xc_ref[...] = xc

    def gates(c, carry):
        t0 = pl.multiple_of(c * CH, CH)
        xcb = xc_ref[pl.ds(t0, CH), :]
        xb = xcb.astype(MM)
        hx = 0.5 * xcb
        for d in range(2):
            tr = jnp.tanh(jnp.dot(xb, wg_ref[0, 2 * d], preferred_element_type=F32) + bg_ref[2 * d:2 * d + 1, :])
            ti = jnp.tanh(jnp.dot(xb, wg_ref[0, 2 * d + 1], preferred_element_type=F32)
                          + bg_ref[2 * d + 1:2 * d + 2, :])
            half_c = (-0.5 * LRU_C) * jax.nn.softplus(-lam_ref[d:d + 1, :])
            log_a = half_c * tr + half_c
            ix = hx * ti + hx
            a = jnp.exp(log_a)
            m2 = jnp.maximum(-jnp.tanh(log_a) * (a * a + 1.0), 0.0)
            mult = jnp.where(m2 > 0.0, m2 * lax.rsqrt(m2), 0.0)
            s0 = pl.multiple_of(c * CHS, 8)
            a_ref[d, pl.ds(s0, CH), :] = a
            b_ref[d, pl.ds(s0, CH), :] = mult * ix
        return carry
    lax.fori_loop(0, NCH, gates, 0)

    def scan(j, carry):
        hf, pf, hb, pb = carry
        jb = CH - 1 - j
        sf = pl.ds(j, NCH, stride=CHS)
        sb = pl.ds(jb, NCH, stride=CHS)
        af = a_ref[0, sf, :]
        bf = b_ref[0, sf, :]
        ab = a_ref[1, sb, :]
        bb = b_ref[1, sb, :]
        hf = af * hf + bf
        pf = af * pf
        hb = ab * hb + bb
        pb = ab * pb
        b_ref[0, sf, :] = hf
        a_ref[0, sf, :] = pf
        b_ref[1, sb, :] = hb
        a_ref[1, sb, :] = pb
        return hf, pf, hb, pb
    z = jnp.zeros((NCH, LANES), F32)
    o = jnp.ones((NCH, LANES), F32)
    hf_end, pf_end, hb_beg, pb_beg = lax.fori_loop(0, CH, scan, (z, o, z, o))

    cur = h0_ref[0, 0, 0:1, :]
    rows_f = []
    for c in range(NCH):
        rows_f.append(cur)
        cur = pf_end[c:c + 1, :] * cur + hf_end[c:c + 1, :]
    cur = h0_ref[0, 0, 1:2, :]
    rows_b = [None] * NCH
    for c in range(NCH - 1, -1, -1):
        rows_b[c] = cur
        cur = pb_beg[c:c + 1, :] * cur + hb_beg[c:c + 1, :]
    hin_f = jnp.where(is_lat, jnp.concatenate(rows_f, axis=0), 0.0)
    hin_b = jnp.where(is_lat, jnp.concatenate(rows_b, axis=0), 0.0)
    hin_ref[0] = hin_f
    hin_ref[1] = hin_b
    st_ref[0, 0] = pf_end * hin_f + hf_end
    st_ref[0, 1] = pb_beg * hin_b + hb_beg

    def finish(c, carry):
        t0 = pl.multiple_of(c * CH, CH)
        rows = pl.ds(pl.multiple_of(c * CHS, 8), CH)
        hf = a_ref[0, rows, :] * hin_ref[0, pl.ds(c, 1), :] + b_ref[0, rows, :]
        hb = a_ref[1, rows, :] * hin_ref[1, pl.ds(c, 1), :] + b_ref[1, rows, :]
        y = jax.nn.gelu(ug_ref[pl.ds(t0, CH), :]) * (hf + hb)
        o_ref[pl.ds(t0, CH), :] = y.astype(MM)
        return carry
    lax.fori_loop(0, NCH, finish, 0)


def _lru(l, u_lru, u_gate, w_conv, b_conv, wg_bd, bg, lam, state):
    nseg = NALL // SEG
    ncb = D_LRU // LANES
    nctx = NTOK // SEG
    return pl.pallas_call(
        _lru_kernel,
        grid=(nseg, ncb),
        in_specs=[
            pl.BlockSpec((SEG, LANES), lambda s, c: (s, c)),
            pl.BlockSpec((SEG, LANES), lambda s, c: (s, c)),
            pl.BlockSpec((4, LANES), lambda s, c: (0, c)),
            pl.BlockSpec((1, LANES), lambda s, c: (0, c)),
            pl.BlockSpec((1, 4, LANES, LANES), lambda s, c: (c, 0, 0, 0)),
            pl.BlockSpec((4, LANES), lambda s, c: (0, c)),
            pl.BlockSpec((2, LANES), lambda s, c: (0, c)),
            pl.BlockSpec((1, 1, 2, LANES), lambda s, c: (jnp.maximum(s - nctx, 0), l, 0, c)),
        ],
        out_specs=[
            pl.BlockSpec((SEG, LANES), lambda s, c: (s, c)),
            pl.BlockSpec((1, 2, NCH, LANES), lambda s, c: (s, 0, 0, c)),
        ],
        out_shape=[
            jax.ShapeDtypeStruct((NALL, D_LRU), MM),
            jax.ShapeDtypeStruct((nseg, 2, NCH, D_LRU), F32),
        ],
        scratch_shapes=[
            pltpu.VMEM((SEG, LANES), F32),
            pltpu.VMEM((2, NCH * CHS, LANES), F32),
            pltpu.VMEM((2, NCH * CHS, LANES), F32),
            pltpu.VMEM((2, NCH, LANES), F32),
        ],
        compiler_params=_cp(("arbitrary", "arbitrary")),
        name="lru",
    )(u_lru, u_gate, w_conv, b_conv.reshape(1, D_LRU), wg_bd, bg, lam, state)


RT = D // LANES


def _store_row_tiles(ref, start, n, v):
    for c in range(RT):
        ref[pl.ds(start * RT + c, n, stride=RT), :] = v[:, c * LANES:(c + 1) * LANES]


def _load_row_tiles(ref, start, n):
    return jnp.concatenate([ref[pl.ds(start * RT + c, n, stride=RT), :] for c in range(RT)], axis=1)


def _outproj_kernel(xa_ref, xb_ref, yp_ref, yl_ref, mod_ref, g_ref, wo_ref, wr_ref, x1_ref, hn_ref, aff_ref):
    mix = jnp.dot(yp_ref[...], wo_ref[0:D_POOL, :], preferred_element_type=F32)
    mix = mix + jnp.dot(yl_ref[...], wo_ref[D_POOL:, :], preferred_element_type=F32)
    x1 = _stream_tile(xa_ref, xb_ref, TM) + mod_ref[0, 2:3, :] * mix
    x1_ref[...] = x1
    hn = _rms(x1, g_ref[...]) * (1.0 + mod_ref[0, 4:5, :]) + mod_ref[0, 3:4, :]
    _store_row_tiles(hn_ref, 0, TM, hn)
    hn_hi = hn.astype(BF16)
    hn_lo = (hn - hn_hi.astype(F32)).astype(BF16)
    nt = (((1,), (1,)), ((), ()))
    logits = (lax.dot_general(wr_ref[0], hn_hi, nt, preferred_element_type=F32)
              + lax.dot_general(wr_ref[0], hn_lo, nt, preferred_element_type=F32)
              + lax.dot_general(wr_ref[1], hn_hi, nt, preferred_element_type=F32))
    e = jnp.exp(logits - jnp.max(logits, axis=0, keepdims=True))
    aff_ref[...] = e / jnp.sum(e, axis=0, keepdims=True)


def _outproj(xa, xb, y_pool, y_lru, mod_l, g, w_out_bf, w_router_t):
    return pl.pallas_call(
        _outproj_kernel,
        grid=(NALL // TM,),
        in_specs=_stream_specs(TM) + [
            pl.BlockSpec((TM, D_POOL), lambda i: (i, 0)),
            pl.BlockSpec((TM, D_LRU), lambda i: (i, 0)),
            pl.BlockSpec((1, 8, D), lambda i: (_mod_row(i), 0, 0)),
            pl.BlockSpec((1, D), lambda i: (0, 0)),
            pl.BlockSpec((D, D), lambda i: (0, 0)),
            pl.BlockSpec((2, N_EXPERTS, D), lambda i: (0, 0, 0)),
        ],
        out_specs=[
            pl.BlockSpec((TM, D), lambda i: (i, 0)),
            pl.BlockSpec((TM * RT, LANES), lambda i: (i, 0)),
            pl.BlockSpec((N_EXPERTS, TM), lambda i: (0, i)),
        ],
        out_shape=[
            jax.ShapeDtypeStruct((NALL, D), F32),
            jax.ShapeDtypeStruct((NALL * RT, LANES), F32),
            jax.ShapeDtypeStruct((N_EXPERTS, NALL), F32),
        ],
        compiler_params=_cp(("arbitrary",)),
        name="outproj",
    )(xa, xb, y_pool, y_lru, mod_l, g.reshape(1, D), w_out_bf, w_router_t)


def _lane_prefix(m, tri):
    parts = []
    bounds = []
    run = jnp.zeros((m.shape[0], 1), F32)
    for k in range(NTOK // LANES):
        if (k * LANES) % TT == 0:
            bounds.append(run)
        p = jnp.dot(m[:, k * LANES:(k + 1) * LANES].astype(BF16), tri, preferred_element_type=F32)
        parts.append(p + run)
        run = run + p[:, LANES - 1:LANES]
    bounds.append(run)
    return jnp.concatenate(parts, axis=1), bounds


def _plan_kernel(aff_ref, slot_ref, kord_ref, cnt_ref, off_ref, maxk_ref):
    aff = aff_ref[...]
    bits = pltpu.bitcast(aff, jnp.int32)

    def count_ge(v):
        return jnp.sum(jnp.where(bits >= v, 1.0, 0.0), axis=1, keepdims=True)

    def bisect(_, c):
        lo, hi = c
        mid = lo + jnp.right_shift(hi - lo + 1, 1)
        ok = count_ge(mid) >= float(CAP)
        return jnp.where(ok, mid, lo), jnp.where(ok, hi, mid - 1)
    lo0 = jnp.zeros((N_EXPERTS, 1), jnp.int32)
    hi0 = jnp.full((N_EXPERTS, 1), 0x7F7FFFFF, jnp.int32)
    thr, _ = lax.fori_loop(0, 31, bisect, (lo0, hi0))

    ri = lax.broadcasted_iota(jnp.int32, (LANES, LANES), 0)
    ci = lax.broadcasted_iota(jnp.int32, (LANES, LANES), 1)
    tri = jnp.where(ri <= ci, 1.0, 0.0).astype(BF16)

    gt = bits > thr
    eq = bits == thr
    need = float(CAP) - jnp.sum(jnp.where(gt, 1.0, 0.0), axis=1, keepdims=True)
    eq_rank, _ = _lane_prefix(jnp.where(eq, 1.0, 0.0), tri)
    sel = jnp.where(gt, 1.0, jnp.where(eq & (eq_rank <= need), 1.0, 0.0))
    rank, bounds = _lane_prefix(sel, tri)
    slot_ref[0] = jnp.where(sel > 0.0, rank - 1.0, -1.0)

    er = lax.broadcasted_iota(jnp.int32, (N_EXPERTS, N_EXPERTS), 0)
    ec = lax.broadcasted_iota(jnp.int32, (N_EXPERTS, N_EXPERTS), 1)
    below = jnp.where(ec < er, 1.0, 0.0).astype(BF16)
    kord_ref[0] = jnp.dot(below, sel.astype(BF16), preferred_element_type=F32)
    cnt = jnp.sum(sel, axis=0, keepdims=True)
    cnt_ref[0] = cnt

    lane = lax.broadcasted_iota(jnp.int32, (N_EXPERTS, LANES), 1)
    offs = jnp.zeros((N_EXPERTS, LANES), F32)
    for b, col in enumerate(bounds):
        offs = jnp.where(lane == b, col, offs)
    off_ref[0] = offs.astype(jnp.int32)

    lane1 = lax.broadcasted_iota(jnp.int32, (1, LANES), 1)
    mk = jnp.zeros((1, LANES), F32)
    for b in range(NTT):
        mk = jnp.where(lane1 == b, jnp.max(cnt[:, b * TT:(b + 1) * TT], axis=1, keepdims=True), mk)
    maxk_ref[0] = mk.astype(jnp.int32)


def _plan(aff_t):
    ns = NALL // NTOK
    return pl.pallas_call(
        _plan_kernel,
        grid=(ns,),
        in_specs=[pl.BlockSpec((N_EXPERTS, NTOK), lambda s: (0, s))],
        out_specs=[
            pl.BlockSpec((1, N_EXPERTS, NTOK), lambda s: (s, 0, 0)),
            pl.BlockSpec((1, N_EXPERTS, NTOK), lambda s: (s, 0, 0)),
            pl.BlockSpec((1, 1, NTOK), lambda s: (s, 0, 0)),
            pl.BlockSpec((1, N_EXPERTS, LANES), lambda s: (s, 0, 0)),
            pl.BlockSpec((1, 1, LANES), lambda s: (s, 0, 0)),
        ],
        out_shape=[
            jax.ShapeDtypeStruct((ns, N_EXPERTS, NTOK), F32),
            jax.ShapeDtypeStruct((ns, N_EXPERTS, NTOK), F32),
            jax.ShapeDtypeStruct((ns, 1, NTOK), F32),
            jax.ShapeDtypeStruct((ns, N_EXPERTS, LANES), jnp.int32),
            jax.ShapeDtypeStruct((ns, 1, LANES), jnp.int32),
        ],
        compiler_params=_cp(("arbitrary",)),
        name="plan",
    )(aff_t)


COMPACT_UNROLL = 16


def _compact_kernel(off_ref, slot_ref, aff_ref, kord_ref, o_ref):
    p = pl.program_id(0)
    nj = CAP // TT
    sub = lax.broadcasted_iota(jnp.int32, (TT, TT), 0).astype(F32)
    lane = lax.broadcasted_iota(jnp.int32, (1, TT), 1).astype(F32)
    lane8 = lax.broadcasted_iota(jnp.int32, (8, TT), 1)

    def tile(b, accs):
        lo = off_ref[p * LANES + b]
        t0 = pl.multiple_of(b * TT, TT)
        srow = slot_ref[0, :, pl.ds(t0, TT)]
        g = aff_ref[0, :, pl.ds(t0, TT)]
        g_hi = g.astype(BF16)
        r1 = g - g_hi.astype(F32)
        g_mid = r1.astype(BF16)
        g_lo = (r1 - g_mid.astype(F32)).astype(BF16)
        tile_id = jnp.full((1, TT), b, jnp.int32).astype(F32)
        data = jnp.concatenate([
            tile_id.astype(BF16), lane.astype(BF16), g_hi, g_mid, g_lo,
            kord_ref[0, :, pl.ds(t0, TT)].astype(BF16),
            jnp.zeros((2, TT), BF16)], axis=0)
        onehot = jnp.where(srow - lo.astype(F32) == sub, 1.0, 0.0).astype(BF16)
        res = lax.dot_general(data, onehot, (((1,), (1,)), ((), ())), preferred_element_type=F32)
        shift = lo & (TT - 1)
        j0 = lo // TT
        rolled = pltpu.roll(res, shift, axis=1)
        first = lane8 >= shift
        out = []
        for j in range(nj):
            mine = jnp.logical_or(jnp.logical_and(first, j == j0),
                                  jnp.logical_and(jnp.logical_not(first), j == j0 + 1))
            out.append(accs[j] + jnp.where(mine, rolled, 0.0))
        return tuple(out)

    def tiles(i, accs):
        for u in range(COMPACT_UNROLL):
            accs = tile(i * COMPACT_UNROLL + u, accs)
        return accs
    zero = jnp.zeros((8, TT), F32)
    accs = lax.fori_loop(0, NTT // COMPACT_UNROLL, tiles, (zero,) * nj)
    for j in range(nj):
        o_ref[0, :, j * TT:(j + 1) * TT] = accs[j]


def _compact(offs, slot, aff_rows, kord):
    npair = (NALL // NTOK) * N_EXPERTS
    spec = pl.BlockSpec((1, 1, NTOK), lambda p, off: (p, 0, 0))
    return pl.pallas_call(
        _compact_kernel,
        grid_spec=pltpu.PrefetchScalarGridSpec(
            num_scalar_prefetch=1,
            grid=(npair,),
            in_specs=[spec, spec, spec],
            out_specs=pl.BlockSpec((1, 8, CAP), lambda p, off: (p, 0, 0)),
        ),
        out_shape=jax.ShapeDtypeStruct((npair, 8, CAP), F32),
        compiler_params=_cp(("arbitrary",)),
        name="compact",
    )(offs.reshape(-1), slot.reshape(npair, 1, NTOK), aff_rows.reshape(npair, 1, NTOK),
      kord.reshape(npair, 1, NTOK))


FSPLIT = 2
NPAIR = (NALL // NTOK) * N_EXPERTS
GATHER_UNROLL = 8


def _ffn_kernel(idx_ref, hn_ref, wg_ref, wu_ref, wd_ref, g_ref, o_ref, xs_a, xs_b, acc_ref, sem):
    p = pl.program_id(0)
    f = pl.program_id(1)
    share = CAP // FSPLIT

    def start_row(pair, dst_ref, dst_sem, s):
        src = pl.multiple_of(idx_ref[pair * CAP + s] * RT, RT)
        pltpu.make_async_copy(hn_ref.at[pl.ds(src, RT)], dst_ref.at[pl.ds(s * RT, RT)], dst_sem).start()

    def wait_rows(ref, ref_sem):
        pltpu.make_async_copy(hn_ref.at[pl.ds(0, CAP * RT)], ref, ref_sem).wait()

    @pl.when(jnp.logical_and(p == 0, f == 0))
    def _():
        acc_ref[...] = jnp.zeros_like(acc_ref)

        def issue(i, c):
            for u in range(GATHER_UNROLL):
                start_row(0, xs_a, sem.at[0], i * GATHER_UNROLL + u)
            return c
        lax.fori_loop(0, CAP // GATHER_UNROLL, issue, 0)

    def step(cur, cur_sem, nxt, nxt_sem):
        @pl.when(f == 0)
        def _():
            wait_rows(cur, cur_sem)

        nxt_pair = jnp.minimum(p + 1, NPAIR - 1)
        for s in range(share):
            start_row(nxt_pair, nxt, nxt_sem, f * share + s)

        x = _load_row_tiles(cur, 0, CAP)
        hg = jnp.dot(x, wg_ref[0, 0], preferred_element_type=F32)
        hu = jnp.dot(x, wu_ref[0, 0], preferred_element_type=F32)
        hid = hg * _sigmoid(hg) * hu
        o = jnp.dot(hid, wd_ref[0, 0], preferred_element_type=F32)

        acc = jnp.where(f > 0, acc_ref[...], 0.0) + o
        acc_ref[...] = acc
        _store_row_tiles(o_ref, 0, CAP, acc * g_ref[0][:, 0:1])

        @pl.when(jnp.logical_and(p == NPAIR - 1, f == FSPLIT - 1))
        def _():
            wait_rows(nxt, nxt_sem)

    @pl.when(p % 2 == 0)
    def _():
        step(xs_a, sem.at[0], xs_b, sem.at[1])

    @pl.when(p % 2 == 1)
    def _():
        step(xs_b, sem.at[1], xs_a, sem.at[0])


def _ffn(l, idx_rows, hn, w_gate, w_up, w_down, g_cols):
    fw = D_EXPERT // FSPLIT
    return pl.pallas_call(
        _ffn_kernel,
        grid_spec=pltpu.PrefetchScalarGridSpec(
            num_scalar_prefetch=1,
            grid=(NPAIR, FSPLIT),
            in_specs=[
                pl.BlockSpec(memory_space=pl.ANY),
                pl.BlockSpec((1, 1, D, fw), lambda p, f, idx: (l, p % N_EXPERTS, 0, f)),
                pl.BlockSpec((1, 1, D, fw), lambda p, f, idx: (l, p % N_EXPERTS, 0, f)),
                pl.BlockSpec((1, 1, fw, D), lambda p, f, idx: (l, p % N_EXPERTS, f, 0)),
                pl.BlockSpec((1, CAP, LANES), lambda p, f, idx: (p, 0, 0)),
            ],
            out_specs=pl.BlockSpec((CAP * RT, LANES), lambda p, f, idx: (p, 0)),
            scratch_shapes=[
                pltpu.VMEM((CAP * RT, LANES), F32),
                pltpu.VMEM((CAP * RT, LANES), F32),
                pltpu.VMEM((CAP, D), F32),
                pltpu.SemaphoreType.DMA((2,)),
            ],
        ),
        out_shape=jax.ShapeDtypeStruct((NPAIR * CAP * RT, LANES), F32),
        compiler_params=_cp(("arbitrary", "arbitrary")),
        name="ffn",
    )(idx_rows, hn, w_gate, w_up, w_down, g_cols)


COMBINE_UNROLL = 8
STAGE_ROWS = N_EXPERTS * TT


def _combine_kernel(off_ref, dst_ref, maxk_ref, x1_ref, mod_ref, cnt_ref, gf_ref, eo_ref,
                    octx_ref, o_ref, stage_ref, sem, *, final):
    i = pl.program_id(0)
    ntile = pl.num_programs(0)
    buf = i % 2

    def fetch_tile(tile, slot):
        s = tile // NTT
        b = tile % NTT
        slot_base = slot * STAGE_ROWS

        def fetch(r):
            dst = pl.multiple_of((slot_base + dst_ref[r]) * RT, RT)
            pltpu.make_async_copy(eo_ref.at[pl.ds(pl.multiple_of(r * RT, RT), RT)], stage_ref.at[pl.ds(dst, RT)],
                                  sem.at[slot]).start()

        for e in range(N_EXPERTS):
            pe = s * N_EXPERTS + e
            lo = off_ref[pe * LANES + b]
            n = off_ref[pe * LANES + b + 1] - lo
            base = pe * CAP + lo
            nq = lax.shift_right_logical(n, COMBINE_UNROLL.bit_length() - 1)

            def issue_many(q, c, base=base):
                for u in range(COMBINE_UNROLL):
                    fetch(base + q * COMBINE_UNROLL + u)
                return c
            lax.fori_loop(0, nq, issue_many, 0)

            def issue_one(r, c):
                fetch(r)
                return c
            lax.fori_loop(base + nq * COMBINE_UNROLL, base + n, issue_one, 0)

    def rows_of_tile(tile):
        s = tile // NTT
        b = tile % NTT
        total = 0
        for e in range(N_EXPERTS):
            pe = s * N_EXPERTS + e
            total = total + off_ref[pe * LANES + b + 1] - off_ref[pe * LANES + b]
        return total

    @pl.when(i == 0)
    def _():
        fetch_tile(i, buf)

    @pl.when(i + 1 < ntile)
    def _():
        fetch_tile(i + 1, 1 - buf)

    total = rows_of_tile(i)

    @pl.when(total > 0)
    def _():
        pltpu.make_async_copy(eo_ref.at[pl.ds(0, total * RT)], stage_ref.at[pl.ds(0, total * RT)],
                              sem.at[buf]).wait()

    cnt = cnt_ref[:, 0:1]
    o_ref[...] = jnp.zeros_like(o_ref)

    def add(k, c):
        rows = _load_row_tiles(stage_ref, pl.multiple_of(buf * STAGE_ROWS + k * TT, TT), TT)
        o_ref[...] += jnp.where(cnt > k.astype(F32), rows, 0.0)
        return c
    lax.fori_loop(0, maxk_ref[(i // NTT) * LANES + i % NTT], add, 0)
    x2 = x1_ref[...] + mod_ref[0, 5:6, :] * o_ref[...]
    if final:
        x2 = _rms(x2, gf_ref[...])
    o_ref[...] = x2

    @pl.when(i < NTT)
    def _():
        octx_ref[...] = x2


def _combine(offs, dst_rows, maxk, x1, mod_l, cnt_cols, g_final, eo, final):
    def mrow(i):
        return jnp.where(i < NTT, 0, 1 + (i - NTT) // (LAT_L // TT))
    return pl.pallas_call(
        functools.partial(_combine_kernel, final=final),
        grid_spec=pltpu.PrefetchScalarGridSpec(
            num_scalar_prefetch=3,
            grid=(NALL // TT,),
            in_specs=[
                pl.BlockSpec((TT, D), lambda i, *_: (i, 0)),
                pl.BlockSpec((1, 8, D), lambda i, *_: (mrow(i), 0, 0)),
                pl.BlockSpec((TT, LANES), lambda i, *_: (i, 0)),
                pl.BlockSpec((1, D), lambda i, *_: (0, 0)),
                pl.BlockSpec(memory_space=pl.ANY),
            ],
            out_specs=_stream_specs(TT),
            scratch_shapes=[pltpu.VMEM((2 * STAGE_ROWS * RT, LANES), F32), pltpu.SemaphoreType.DMA((2,))],
        ),
        out_shape=[jax.ShapeDtypeStruct((NTOK, D), F32)] * 2,
        compiler_params=_cp(("arbitrary",)),
        name="combine",
    )(offs.reshape(-1), dst_rows, maxk.reshape(-1), x1, mod_l, cnt_cols, g_final.reshape(1, D), eo)


def _gate_blocks(w_a, w_x):
    hd = w_a.shape[-1]
    per = LANES // hd
    ncb = D_LRU // LANES
    mats = jnp.stack([w_a[0], w_x[0], w_a[1], w_x[1]], axis=0)
    mats = mats.reshape(4, ncb, per, hd, hd)
    out = jnp.zeros((ncb, 4, LANES, LANES), F32)
    for q in range(per):
        out = out.at[:, :, q * hd:(q + 1) * hd, q * hd:(q + 1) * hd].set(mats[:, :, q].transpose(1, 0, 2, 3))
    return out.astype(MM)


def kernel(x_prompt, x_sample, state_rglru, c, c_ctx, w_mod, b_mod, g_norm_mix, g_norm_ffn, w_in, w_pool,
           pool_scale, w_conv, b_conv, w_gate_a, b_gate_a, w_gate_x, b_gate_x, lru_lambda, w_out, w_router,
           w_exp_gate, w_exp_up, w_exp_down, g_final):
    nb_ctx = x_prompt.shape[0]
    nb_lat = x_sample.shape[0]
    xa, xb = x_prompt.reshape(NTOK, D), x_sample.reshape(NTOK, D)

    cvec = jnp.zeros((8, D), F32).at[0].set(c_ctx).at[1:1 + nb_lat].set(c)
    mod = _modulation(cvec, w_mod, b_mod)
    mod = jnp.pad(mod.transpose(0, 2, 1, 3), ((0, 0), (0, 0), (0, 2), (0, 0)))

    states = []
    for l in range(DEPTH):
        u_pool, u_lru, u_gate = _inproj(xa, xb, mod[l], g_norm_mix[l], w_in[l].astype(MM))
        y_pool = _pool(u_pool, w_pool[l].astype(MM), pool_scale[l])
        bg = jnp.stack([b_gate_a[l, 0], b_gate_x[l, 0], b_gate_a[l, 1], b_gate_x[l, 1]], axis=0).reshape(4, D_LRU)
        y_lru, st = _lru(l, u_lru, u_gate, w_conv[l], b_conv[l], 0.5 * _gate_blocks(w_gate_a[l], w_gate_x[l]),
                         0.5 * bg, lru_lambda[l], state_rglru)
        states.append(st[:NTOK // SEG].transpose(0, 2, 1, 3).reshape(nb_ctx, 2, D_LRU))

        wr = w_router[l].T
        wr_hi = wr.astype(BF16)
        wr_split = jnp.stack([wr_hi, (wr - wr_hi.astype(F32)).astype(BF16)], axis=0)
        x1, hn, aff_t = _outproj(xa, xb, y_pool, y_lru, mod[l], g_norm_ffn[l], w_out[l].astype(MM), wr_split)
        slot, kord, cnt, offs, maxk = _plan(aff_t)
        aff_rows = aff_t.reshape(N_EXPERTS, NALL // NTOK, NTOK).transpose(1, 0, 2)
        lists = _compact(offs, slot, aff_rows, kord)
        tok_local = (lists[:, 0] * TT + lists[:, 1]).astype(jnp.int32)
        g_sel = lists[:, 2] + lists[:, 3] + lists[:, 4]
        kord_i = lists[:, 5].astype(jnp.int32)
        stream_base = (jnp.arange(NPAIR, dtype=jnp.int32) // N_EXPERTS) * NTOK
        src_rows = (tok_local + stream_base[:, None]).reshape(-1)
        g_cols = jnp.broadcast_to(g_sel[:, :, None], g_sel.shape + (LANES,))
        eo = _ffn(l, src_rows, hn, w_exp_gate, w_exp_up, w_exp_down, g_cols)
        cnt_cols = jnp.broadcast_to(cnt.reshape(NALL, 1), (NALL, LANES))
        dst_rows = (kord_i * TT + tok_local % TT).reshape(-1)
        xa, xb = _combine(offs, dst_rows, maxk, x1, mod[l], cnt_cols, g_final, eo, final=(l == DEPTH - 1))

    return xa.reshape(x_prompt.shape), xb.reshape(x_sample.shape), jnp.stack(states, axis=1)
```

```python
import functools

import jax
import jax.numpy as jnp
from jax import lax
from jax.experimental import pallas as pl
from jax.experimental.pallas import tpu as pltpu

F32 = jnp.float32
BF16 = jnp.bfloat16
MM = jnp.float32
HIGHEST = lax.Precision.HIGHEST

D = 1024
NTOK = 8192
NALL = 2 * NTOK
CTX_L = 256
LAT_L = 4096
GRID_W = 64
DEPTH = 2
D_POOL = 512
D_LRU = 512
D_IN = D_POOL + 2 * D_LRU
POOL_WINDOWS = (2, 4, 8, 16)
LANES = 128
N_EXPERTS = 16
CAP = 2 * NTOK // N_EXPERTS
D_EXPERT = 1024
LRU_C = 8.0
EPS = 1e-6

TM = 512
SEG = 4096
CH = 256
NCH = SEG // CH
CHS = CH + 8
TT = 256
NTT = NTOK // TT
VMEM_LIMIT = 56 * 1024 * 1024


def _cp(sem, vmem=VMEM_LIMIT):
    return pltpu.CompilerParams(dimension_semantics=sem, vmem_limit_bytes=vmem)


def _sigmoid(x):
    return 0.5 * jnp.tanh(0.5 * x) + 0.5


def _mod_row(i):
    n_ctx = NTOK // TM
    return jnp.where(i < n_ctx, 0, 1 + (i - n_ctx) // (LAT_L // TM))


def _mod_kernel(c_ref, w_ref, b_ref, o_ref):
    cv = c_ref[...]
    s = cv * jax.nn.sigmoid(cv)
    o_ref[0, 0] = jnp.dot(s, w_ref[0], precision=HIGHEST, preferred_element_type=F32) + b_ref[0, 0]


def _modulation(cvec, w_mod, b_mod):
    return pl.pallas_call(
        _mod_kernel,
        grid=(DEPTH, 6),
        in_specs=[
            pl.BlockSpec((8, D), lambda l, j: (0, 0)),
            pl.BlockSpec((1, D, D), lambda l, j: (l, 0, j)),
            pl.BlockSpec((1, 1, 1, D), lambda l, j: (l, j, 0, 0)),
        ],
        out_specs=pl.BlockSpec((1, 1, 8, D), lambda l, j: (l, j, 0, 0)),
        out_shape=jax.ShapeDtypeStruct((DEPTH, 6, 8, D), F32),
        compiler_params=_cp(("arbitrary", "arbitrary")),
        name="modulation",
    )(cvec, w_mod, b_mod.reshape(DEPTH, 6, 1, D))


def _rms(x, g):
    return x * lax.rsqrt(jnp.mean(x * x, axis=-1, keepdims=True) + EPS) * g


def _stream_specs(tile):
    n = NTOK // tile
    return [pl.BlockSpec((tile, D), lambda i, *_: (jnp.minimum(i, n - 1), 0)),
            pl.BlockSpec((tile, D), lambda i, *_: (jnp.maximum(i - n, 0), 0))]


def _stream_tile(xa_ref, xb_ref, tile):
    return jnp.where(pl.program_id(0) < NTOK // tile, xa_ref[...], xb_ref[...])


def _inproj_kernel(xa_ref, xb_ref, mod_ref, g_ref, w_ref, up_ref, ul_ref, ug_ref):
    x = _stream_tile(xa_ref, xb_ref, TM)
    hn = _rms(x, g_ref[...]) * (1.0 + mod_ref[0, 1:2, :]) + mod_ref[0, 0:1, :]
    proj = jnp.dot(hn.astype(MM), w_ref[...], preferred_element_type=F32)
    up_ref[...] = proj[:, :D_POOL]
    ul_ref[...] = proj[:, D_POOL:D_POOL + D_LRU]
    ug_ref[...] = proj[:, D_POOL + D_LRU:]


def _inproj(xa, xb, mod_l, g, w_in_bf):
    out = jax.ShapeDtypeStruct((NALL, D_POOL), F32)
    return pl.pallas_call(
        _inproj_kernel,
        grid=(NALL // TM,),
        in_specs=_stream_specs(TM) + [
            pl.BlockSpec((1, 8, D), lambda i: (_mod_row(i), 0, 0)),
            pl.BlockSpec((1, D), lambda i: (0, 0)),
            pl.BlockSpec((D, D_IN), lambda i: (0, 0)),
        ],
        out_specs=[pl.BlockSpec((TM, D_POOL), lambda i: (i, 0))] * 3,
        out_shape=[out, out, out],
        compiler_params=_cp(("arbitrary",)),
        name="inproj",
    )(xa, xb, mod_l, g.reshape(1, D), w_in_bf)


def _window_sum(x, h, pos, period):
    n = x.shape[0]
    acc = x
    for i in range(1, h):
        acc = acc + jnp.where(pos < period - i, pltpu.roll(x, n - i, axis=0), 0.0)
    for i in range(1, h + 1):
        acc = acc + jnp.where(pos >= i, pltpu.roll(x, i, axis=0), 0.0)
    return acc


def _window_count(pos, h, period):
    return (jnp.minimum(pos + h, period) - jnp.maximum(pos - h, 0)).astype(F32)


def _pool_kernel(u_ref, w_ref, s_ref, o_ref, pad_ref):
    seg = pl.program_id(0)
    row = lax.broadcasted_iota(jnp.int32, (CH, LANES), 0)
    halo = 8 * GRID_W

    def project(d, g, t0):
        y = jnp.dot(d.astype(MM), w_ref[g], preferred_element_type=F32)
        y = y * s_ref[:, g * LANES:(g + 1) * LANES]
        o_ref[pl.ds(t0, CH), g * LANES:(g + 1) * LANES] = y.astype(MM)

    @pl.when(seg < NTOK // SEG)
    def _context():
        def body(sb, c):
            t0 = pl.multiple_of(sb * CH, CH)
            for g, w in enumerate(POOL_WINDOWS):
                h = w // 2
                x = u_ref[pl.ds(t0, CH), g * LANES:(g + 1) * LANES]
                m = _window_sum(x, h, row, CTX_L) / _window_count(row, h, CTX_L)
                project(m - x, g, t0)
            return c
        lax.fori_loop(0, NCH, body, 0)

    @pl.when(seg >= NTOK // SEG)
    def _latent():
        zeros = jnp.zeros((halo, D_POOL), F32)
        pad_ref[0:halo, :] = zeros
        pad_ref[halo + SEG:halo + SEG + halo, :] = zeros
        col = row & (GRID_W - 1)

        def cols(sb, c):
            t0 = pl.multiple_of(sb * CH, CH)
            for g, w in enumerate(POOL_WINDOWS):
                h = w // 2
                x = u_ref[pl.ds(t0, CH), g * LANES:(g + 1) * LANES]
                m = _window_sum(x, h, col, GRID_W) / _window_count(col, h, GRID_W)
                pad_ref[pl.ds(halo + t0, CH), g * LANES:(g + 1) * LANES] = m
            return c
        lax.fori_loop(0, NCH, cols, 0)

        def rows(sb, c):
            t0 = pl.multiple_of(sb * CH, CH)
            r = jnp.right_shift(t0 + row, GRID_W.bit_length() - 1)
            for g, w in enumerate(POOL_WINDOWS):
                h = w // 2
                acc = None
                for i in range(-h, h):
                    start = pl.multiple_of(halo + t0 + i * GRID_W, GRID_W)
                    v = pad_ref[pl.ds(start, CH), g * LANES:(g + 1) * LANES]
                    acc = v if acc is None else acc + v
                m = acc / _window_count(r, h, GRID_W)
                x = u_ref[pl.ds(t0, CH), g * LANES:(g + 1) * LANES]
                project(m - x, g, t0)
            return c
        lax.fori_loop(0, NCH, rows, 0)


def _pool(u_pool, w_pool_bf, pool_scale):
    halo = 8 * GRID_W
    return pl.pallas_call(
        _pool_kernel,
        grid=(NALL // SEG,),
        in_specs=[
            pl.BlockSpec((SEG, D_POOL), lambda i: (i, 0)),
            pl.BlockSpec((len(POOL_WINDOWS), LANES, LANES), lambda i: (0, 0, 0)),
            pl.BlockSpec((1, D_POOL), lambda i: (0, 0)),
        ],
        out_specs=pl.BlockSpec((SEG, D_POOL), lambda i: (i, 0)),
        out_shape=jax.ShapeDtypeStruct((NALL, D_POOL), MM),
        scratch_shapes=[pltpu.VMEM((SEG + 2 * halo, D_POOL), F32)],
        compiler_params=_cp(("arbitrary",)),
        name="pool",
    )(u_pool, w_pool_bf, pool_scale.reshape(1, D_POOL))


def _lru_kernel(ul_ref, ug_ref, wc_ref, bc_ref, wg_ref, bg_ref, lam_ref, h0_ref,
                o_ref, st_ref, xc_ref, a_ref, b_ref, hin_ref):
    seg = pl.program_id(0)
    is_lat = seg >= NTOK // SEG
    last = jnp.where(is_lat, LAT_L - 1, CTX_L - 1)

    x = ul_ref[...]
    pos = lax.broadcasted_iota(jnp.int32, (SEG, LANES), 0) & last
    xc = bc_ref[...] + wc_ref[0:1, :] * jnp.where(pos >= 2, pltpu.roll(x, 2, axis=0), 0.0)
    xc = xc + wc_ref[1:2, :] * jnp.where(pos >= 1, pltpu.roll(x, 1, axis=0), 0.0)
    xc = xc + wc_ref[2:3, :] * x
    xc = xc + wc_ref[3:4, :] * jnp.where(pos < last, pltpu.roll(x, SEG - 1, axis=0), 0.0)
    xc_ref[...] = xc

    def gates(c, carry):
        t0 = pl.multiple_of(c * CH, CH)
        xcb = xc_ref[pl.ds(t0, CH), :]
        xb = xcb.astype(MM)
        hx = 0.5 * xcb
        for d in range(2):
            tr = jnp.tanh(jnp.dot(xb, wg_ref[0, 2 * d], preferred_element_type=F32) + bg_ref[2 * d:2 * d + 1, :])
            ti = jnp.tanh(jnp.dot(xb, wg_ref[0, 2 * d + 1], preferred_element_type=F32)
                          + bg_ref[2 * d + 1:2 * d + 2, :])
            half_c = (-0.5 * LRU_C) * jax.nn.softplus(-lam_ref[d:d + 1, :])
            log_a = half_c * tr + half_c
            ix = hx * ti + hx
            a = jnp.exp(log_a)
            m2 = jnp.maximum(-jnp.tanh(log_a) * (a * a + 1.0), 0.0)
            mult = jnp.where(m2 > 0.0, m2 * lax.rsqrt(m2), 0.0)
            s0 = pl.multiple_of(c * CHS, 8)
            a_ref[d, pl.ds(s0, CH), :] = a
            b_ref[d, pl.ds(s0, CH), :] = mult * ix
        return carry
    lax.fori_loop(0, NCH, gates, 0)

    def scan(j, carry):
        hf, pf, hb, pb = carry
        jb = CH - 1 - j
        sf = pl.ds(j, NCH, stride=CHS)
        sb = pl.ds(jb, NCH, stride=CHS)
        af = a_ref[0, sf, :]
        bf = b_ref[0, sf, :]
        ab = a_ref[1, sb, :]
        bb = b_ref[1, sb, :]
        hf = af * hf + bf
        pf = af * pf
        hb = ab * hb + bb
        pb = ab * pb
        b_ref[0, sf, :] = hf
        a_ref[0, sf, :] = pf
        b_ref[1, sb, :] = hb
        a_ref[1, sb, :] = pb
        return hf, pf, hb, pb
    z = jnp.zeros((NCH, LANES), F32)
    o = jnp.ones((NCH, LANES), F32)
    hf_end, pf_end, hb_beg, pb_beg = lax.fori_loop(0, CH, scan, (z, o, z, o))

    cur = h0_ref[0, 0, 0:1, :]
    rows_f = []
    for c in range(NCH):
        rows_f.append(cur)
        cur = pf_end[c:c + 1, :] * cur + hf_end[c:c + 1, :]
    cur = h0_ref[0, 0, 1:2, :]
    rows_b = [None] * NCH
    for c in range(NCH - 1, -1, -1):
        rows_b[c] = cur
        cur = pb_beg[c:c + 1, :] * cur + hb_beg[c:c + 1, :]
    hin_f = jnp.where(is_lat, jnp.concatenate(rows_f, axis=0), 0.0)
    hin_b = jnp.where(is_lat, jnp.concatenate(rows_b, axis=0), 0.0)
    hin_ref[0] = hin_f
    hin_ref[1] = hin_b
    st_ref[0, 0] = pf_end * hin_f + hf_end
    st_ref[0, 1] = pb_beg * hin_b + hb_beg

    def finish(c, carry):
        t0 = pl.multiple_of(c * CH, CH)
        rows = pl.ds(pl.multiple_of(c * CHS, 8), CH)
        hf = a_ref[0, rows, :] * hin_ref[0, pl.ds(c, 1), :] + b_ref[0, rows, :]
        hb = a_ref[1, rows, :] * hin_ref[1, pl.ds(c, 1), :] + b_ref[1, rows, :]
        y = jax.nn.gelu(ug_ref[pl.ds(t0, CH), :]) * (hf + hb)
        o_ref[pl.ds(t0, CH), :] = y.astype(MM)
        return carry
    lax.fori_loop(0, NCH, finish, 0)


def _lru(l, u_lru, u_gate, w_conv, b_conv, wg_bd, bg, lam, state):
    nseg = NALL // SEG
    ncb = D_LRU // LANES
    nctx = NTOK // SEG
    return pl.pallas_call(
        _lru_kernel,
        grid=(nseg, ncb),
        in_specs=[
            pl.BlockSpec((SEG, LANES), lambda s, c: (s, c)),
            pl.BlockSpec((SEG, LANES), lambda s, c: (s, c)),
            pl.BlockSpec((4, LANES), lambda s, c: (0, c)),
            pl.BlockSpec((1, LANES), lambda s, c: (0, c)),
            pl.BlockSpec((1, 4, LANES, LANES), lambda s, c: (c, 0, 0, 0)),
            pl.BlockSpec((4, LANES), lambda s, c: (0, c)),
            pl.BlockSpec((2, LANES), lambda s, c: (0, c)),
            pl.BlockSpec((1, 1, 2, LANES), lambda s, c: (jnp.maximum(s - nctx, 0), l, 0, c)),
        ],
        out_specs=[
            pl.BlockSpec((SEG, LANES), lambda s, c: (s, c)),
            pl.BlockSpec((1, 2, NCH, LANES), lambda s, c: (s, 0, 0, c)),
        ],
        out_shape=[
            jax.ShapeDtypeStruct((NALL, D_LRU), MM),
            jax.ShapeDtypeStruct((nseg, 2, NCH, D_LRU), F32),
        ],
        scratch_shapes=[
            pltpu.VMEM((SEG, LANES), F32),
            pltpu.VMEM((2, NCH * CHS, LANES), F32),
            pltpu.VMEM((2, NCH * CHS, LANES), F32),
            pltpu.VMEM((2, NCH, LANES), F32),
        ],
        compiler_params=_cp(("arbitrary", "arbitrary")),
        name="lru",
    )(u_lru, u_gate, w_conv, b_conv.reshape(1, D_LRU), wg_bd, bg, lam, state)


RT = D // LANES


def _store_row_tiles(ref, start, n, v):
    for c in range(RT):
        ref[pl.ds(start * RT + c, n, stride=RT), :] = v[:, c * LANES:(c + 1) * LANES]


def _load_row_tiles(ref, start, n):
    return jnp.concatenate([ref[pl.ds(start * RT + c, n, stride=RT), :] for c in range(RT)], axis=1)


def _outproj_kernel(xa_ref, xb_ref, yp_ref, yl_ref, mod_ref, g_ref, wo_ref, wr_ref, x1_ref, hn_ref, aff_ref):
    mix = jnp.dot(yp_ref[...], wo_ref[0:D_POOL, :], preferred_element_type=F32)
    mix = mix + jnp.dot(yl_ref[...], wo_ref[D_POOL:, :], preferred_element_type=F32)
    x1 = _stream_tile(xa_ref, xb_ref, TM) + mod_ref[0, 2:3, :] * mix
    x1_ref[...] = x1
    hn = _rms(x1, g_ref[...]) * (1.0 + mod_ref[0, 4:5, :]) + mod_ref[0, 3:4, :]
    _store_row_tiles(hn_ref, 0, TM, hn)
    hn_hi = hn.astype(BF16)
    hn_lo = (hn - hn_hi.astype(F32)).astype(BF16)
    nt = (((1,), (1,)), ((), ()))
    logits = (lax.dot_general(wr_ref[0], hn_hi, nt, preferred_element_type=F32)
              + lax.dot_general(wr_ref[0], hn_lo, nt, preferred_element_type=F32)
              + lax.dot_general(wr_ref[1], hn_hi, nt, preferred_element_type=F32))
    e = jnp.exp(logits - jnp.max(logits, axis=0, keepdims=True))
    aff_ref[...] = e / jnp.sum(e, axis=0, keepdims=True)


def _outproj(xa, xb, y_pool, y_lru, mod_l, g, w_out_bf, w_router_t):
    return pl.pallas_call(
        _outproj_kernel,
        grid=(NALL // TM,),
        in_specs=_stream_specs(TM) + [
            pl.BlockSpec((TM, D_POOL), lambda i: (i, 0)),
            pl.BlockSpec((TM, D_LRU), lambda i: (i, 0)),
            pl.BlockSpec((1, 8, D), lambda i: (_mod_row(i), 0, 0)),
            pl.BlockSpec((1, D), lambda i: (0, 0)),
            pl.BlockSpec((D, D), lambda i: (0, 0)),
            pl.BlockSpec((2, N_EXPERTS, D), lambda i: (0, 0, 0)),
        ],
        out_specs=[
            pl.BlockSpec((TM, D), lambda i: (i, 0)),
            pl.BlockSpec((TM * RT, LANES), lambda i: (i, 0)),
            pl.BlockSpec((N_EXPERTS, TM), lambda i: (0, i)),
        ],
        out_shape=[
            jax.ShapeDtypeStruct((NALL, D), F32),
            jax.ShapeDtypeStruct((NALL * RT, LANES), F32),
            jax.ShapeDtypeStruct((N_EXPERTS, NALL), F32),
        ],
        compiler_params=_cp(("arbitrary",)),
        name="outproj",
    )(xa, xb, y_pool, y_lru, mod_l, g.reshape(1, D), w_out_bf, w_router_t)


def _lane_prefix(m, tri):
    parts = []
    bounds = []
    run = jnp.zeros((m.shape[0], 1), F32)
    for k in range(NTOK // LANES):
        if (k * LANES) % TT == 0:
            bounds.append(run)
        p = jnp.dot(m[:, k * LANES:(k + 1) * LANES].astype(BF16), tri, preferred_element_type=F32)
        parts.append(p + run)
        run = run + p[:, LANES - 1:LANES]
    bounds.append(run)
    return jnp.concatenate(parts, axis=1), bounds


def _plan_kernel(aff_ref, slot_ref, kord_ref, cnt_ref, off_ref, maxk_ref):
    aff = aff_ref[...]
    bits = pltpu.bitcast(aff, jnp.int32)

    def count_ge(v):
        return jnp.sum(jnp.where(bits >= v, 1.0, 0.0), axis=1, keepdims=True)

    def bisect(_, c):
        lo, hi = c
        mid = lo + jnp.right_shift(hi - lo + 1, 1)
        ok = count_ge(mid) >= float(CAP)
        return jnp.where(ok, mid, lo), jnp.where(ok, hi, mid - 1)
    lo0 = jnp.zeros((N_EXPERTS, 1), jnp.int32)
    hi0 = jnp.full((N_EXPERTS, 1), 0x7F7FFFFF, jnp.int32)
    thr, _ = lax.fori_loop(0, 31, bisect, (lo0, hi0))

    ri = lax.broadcasted_iota(jnp.int32, (LANES, LANES), 0)
    ci = lax.broadcasted_iota(jnp.int32, (LANES, LANES), 1)
    tri = jnp.where(ri <= ci, 1.0, 0.0).astype(BF16)

    gt = bits > thr
    eq = bits == thr
    need = float(CAP) - jnp.sum(jnp.where(gt, 1.0, 0.0), axis=1, keepdims=True)
    eq_rank, _ = _lane_prefix(jnp.where(eq, 1.0, 0.0), tri)
    sel = jnp.where(gt, 1.0, jnp.where(eq & (eq_rank <= need), 1.0, 0.0))
    rank, bounds = _lane_prefix(sel, tri)
    slot_ref[0] = jnp.where(sel > 0.0, rank - 1.0, -1.0)

    er = lax.broadcasted_iota(jnp.int32, (N_EXPERTS, N_EXPERTS), 0)
    ec = lax.broadcasted_iota(jnp.int32, (N_EXPERTS, N_EXPERTS), 1)
    below = jnp.where(ec < er, 1.0, 0.0).astype(BF16)
    kord_ref[0] = jnp.dot(below, sel.astype(BF16), preferred_element_type=F32)
    cnt = jnp.sum(sel, axis=0, keepdims=True)
    cnt_ref[0] = cnt

    lane = lax.broadcasted_iota(jnp.int32, (N_EXPERTS, LANES), 1)
    offs = jnp.zeros((N_EXPERTS, LANES), F32)
    for b, col in enumerate(bounds):
        offs = jnp.where(lane == b, col, offs)
    off_ref[0] = offs.astype(jnp.int32)

    lane1 = lax.broadcasted_iota(jnp.int32, (1, LANES), 1)
    mk = jnp.zeros((1, LANES), F32)
    for b in range(NTT):
        mk = jnp.where(lane1 == b, jnp.max(cnt[:, b * TT:(b + 1) * TT], axis=1, keepdims=True), mk)
    maxk_ref[0] = mk.astype(jnp.int32)


def _plan(aff_t):
    ns = NALL // NTOK
    return pl.pallas_call(
        _plan_kernel,
        grid=(ns,),
        in_specs=[pl.BlockSpec((N_EXPERTS, NTOK), lambda s: (0, s))],
        out_specs=[
            pl.BlockSpec((1, N_EXPERTS, NTOK), lambda s: (s, 0, 0)),
            pl.BlockSpec((1, N_EXPERTS, NTOK), lambda s: (s, 0, 0)),
            pl.BlockSpec((1, 1, NTOK), lambda s: (s, 0, 0)),
            pl.BlockSpec((1, N_EXPERTS, LANES), lambda s: (s, 0, 0)),
            pl.BlockSpec((1, 1, LANES), lambda s: (s, 0, 0)),
        ],
        out_shape=[
            jax.ShapeDtypeStruct((ns, N_EXPERTS, NTOK), F32),
            jax.ShapeDtypeStruct((ns, N_EXPERTS, NTOK), F32),
            jax.ShapeDtypeStruct((ns, 1, NTOK), F32),
            jax.ShapeDtypeStruct((ns, N_EXPERTS, LANES), jnp.int32),
            jax.ShapeDtypeStruct((ns, 1, LANES), jnp.int32),
        ],
        compiler_params=_cp(("arbitrary",)),
        name="plan",
    )(aff_t)


COMPACT_UNROLL = 16


def _compact_kernel(off_ref, slot_ref, aff_ref, kord_ref, o_ref):
    p = pl.program_id(0)
    nj = CAP // TT
    sub = lax.broadcasted_iota(jnp.int32, (TT, TT), 0).astype(F32)
    lane = lax.broadcasted_iota(jnp.int32, (1, TT), 1).astype(F32)
    lane8 = lax.broadcasted_iota(jnp.int32, (8, TT), 1)

    def tile(b, accs):
        lo = off_ref[p * LANES + b]
        t0 = pl.multiple_of(b * TT, TT)
        srow = slot_ref[0, :, pl.ds(t0, TT)]
        g = aff_ref[0, :, pl.ds(t0, TT)]
        g_hi = g.astype(BF16)
        r1 = g - g_hi.astype(F32)
        g_mid = r1.astype(BF16)
        g_lo = (r1 - g_mid.astype(F32)).astype(BF16)
        tile_id = jnp.full((1, TT), b, jnp.int32).astype(F32)
        data = jnp.concatenate([
            tile_id.astype(BF16), lane.astype(BF16), g_hi, g_mid, g_lo,
            kord_ref[0, :, pl.ds(t0, TT)].astype(BF16),
            jnp.zeros((2, TT), BF16)], axis=0)
        onehot = jnp.where(srow - lo.astype(F32) == sub, 1.0, 0.0).astype(BF16)
        res = lax.dot_general(data, onehot, (((1,), (1,)), ((), ())), preferred_element_type=F32)
        shift = lo & (TT - 1)
        j0 = lo // TT
        rolled = pltpu.roll(res, shift, axis=1)
        first = lane8 >= shift
        out = []
        for j in range(nj):
            mine = jnp.logical_or(jnp.logical_and(first, j == j0),
                                  jnp.logical_and(jnp.logical_not(first), j == j0 + 1))
            out.append(accs[j] + jnp.where(mine, rolled, 0.0))
        return tuple(out)

    def tiles(i, accs):
        for u in range(COMPACT_UNROLL):
            accs = tile(i * COMPACT_UNROLL + u, accs)
        return accs
    zero = jnp.zeros((8, TT), F32)
    accs = lax.fori_loop(0, NTT // COMPACT_UNROLL, tiles, (zero,) * nj)
    for j in range(nj):
        o_ref[0, :, j * TT:(j + 1) * TT] = accs[j]


def _compact(offs, slot, aff_rows, kord):
    npair = (NALL // NTOK) * N_EXPERTS
    spec = pl.BlockSpec((1, 1, NTOK), lambda p, off: (p, 0, 0))
    return pl.pallas_call(
        _compact_kernel,
        grid_spec=pltpu.PrefetchScalarGridSpec(
            num_scalar_prefetch=1,
            grid=(npair,),
            in_specs=[spec, spec, spec],
            out_specs=pl.BlockSpec((1, 8, CAP), lambda p, off: (p, 0, 0)),
        ),
        out_shape=jax.ShapeDtypeStruct((npair, 8, CAP), F32),
        compiler_params=_cp(("arbitrary",)),
        name="compact",
    )(offs.reshape(-1), slot.reshape(npair, 1, NTOK), aff_rows.reshape(npair, 1, NTOK),
      kord.reshape(npair, 1, NTOK))


FSPLIT = 2
NPAIR = (NALL // NTOK) * N_EXPERTS
GATHER_UNROLL = 8


def _ffn_kernel(idx_ref, hn_ref, wg_ref, wu_ref, wd_ref, g_ref, o_ref, xs_a, xs_b, acc_ref, sem):
    p = pl.program_id(0)
    f = pl.program_id(1)
    share = CAP // FSPLIT

    def start_row(pair, dst_ref, dst_sem, s):
        src = pl.multiple_of(idx_ref[pair * CAP + s] * RT, RT)
        pltpu.make_async_copy(hn_ref.at[pl.ds(src, RT)], dst_ref.at[pl.ds(s * RT, RT)], dst_sem).start()

    def wait_rows(ref, ref_sem):
        pltpu.make_async_copy(hn_ref.at[pl.ds(0, CAP * RT)], ref, ref_sem).wait()

    @pl.when(jnp.logical_and(p == 0, f == 0))
    def _():
        acc_ref[...] = jnp.zeros_like(acc_ref)

        def issue(i, c):
            for u in range(GATHER_UNROLL):
                start_row(0, xs_a, sem.at[0], i * GATHER_UNROLL + u)
            return c
        lax.fori_loop(0, CAP // GATHER_UNROLL, issue, 0)

    def step(cur, cur_sem, nxt, nxt_sem):
        @pl.when(f == 0)
        def _():
            wait_rows(cur, cur_sem)

        nxt_pair = jnp.minimum(p + 1, NPAIR - 1)
        for s in range(share):
            start_row(nxt_pair, nxt, nxt_sem, f * share + s)

        x = _load_row_tiles(cur, 0, CAP)
        hg = jnp.dot(x, wg_ref[0, 0], preferred_element_type=F32)
        hu = jnp.dot(x, wu_ref[0, 0], preferred_element_type=F32)
        hid = hg * _sigmoid(hg) * hu
        o = jnp.dot(hid, wd_ref[0, 0], preferred_element_type=F32)

        acc = jnp.where(f > 0, acc_ref[...], 0.0) + o
        acc_ref[...] = acc
        _store_row_tiles(o_ref, 0, CAP, acc * g_ref[0][:, 0:1])

        @pl.when(jnp.logical_and(p == NPAIR - 1, f == FSPLIT - 1))
        def _():
            wait_rows(nxt, nxt_sem)

    @pl.when(p % 2 == 0)
    def _():
        step(xs_a, sem.at[0], xs_b, sem.at[1])

    @pl.when(p % 2 == 1)
    def _():
        step(xs_b, sem.at[1], xs_a, sem.at[0])


def _ffn(l, idx_rows, hn, w_gate, w_up, w_down, g_cols):
    fw = D_EXPERT // FSPLIT
    return pl.pallas_call(
        _ffn_kernel,
        grid_spec=pltpu.PrefetchScalarGridSpec(
            num_scalar_prefetch=1,
            grid=(NPAIR, FSPLIT),
            in_specs=[
                pl.BlockSpec(memory_space=pl.ANY),
                pl.BlockSpec((1, 1, D, fw), lambda p, f, idx: (l, p % N_EXPERTS, 0, f)),
                pl.BlockSpec((1, 1, D, fw), lambda p, f, idx: (l, p % N_EXPERTS, 0, f)),
                pl.BlockSpec((1, 1, fw, D), lambda p, f, idx: (l, p % N_EXPERTS, f, 0)),
                pl.BlockSpec((1, CAP, LANES), lambda p, f, idx: (p, 0, 0)),
            ],
            out_specs=pl.BlockSpec((CAP * RT, LANES), lambda p, f, idx: (p, 0)),
            scratch_shapes=[
                pltpu.VMEM((CAP * RT, LANES), F32),
                pltpu.VMEM((CAP * RT, LANES), F32),
                pltpu.VMEM((CAP, D), F32),
                pltpu.SemaphoreType.DMA((2,)),
            ],
        ),
        out_shape=jax.ShapeDtypeStruct((NPAIR * CAP * RT, LANES), F32),
        compiler_params=_cp(("arbitrary", "arbitrary")),
        name="ffn",
    )(idx_rows, hn, w_gate, w_up, w_down, g_cols)


COMBINE_UNROLL = 8
STAGE_ROWS = N_EXPERTS * TT


def _combine_kernel(off_ref, dst_ref, maxk_ref, x1_ref, mod_ref, cnt_ref, gf_ref, eo_ref,
                    octx_ref, o_ref, stage_ref, sem, *, final):
    i = pl.program_id(0)
    ntile = pl.num_programs(0)
    buf = i % 2

    def fetch_tile(tile, slot):
        s = tile // NTT
        b = tile % NTT
        slot_base = slot * STAGE_ROWS

        def fetch(r, priority=0):
            dst = pl.multiple_of((slot_base + dst_ref[r]) * RT, RT)
            pltpu.make_async_copy(eo_ref.at[pl.ds(pl.multiple_of(r * RT, RT), RT)], stage_ref.at[pl.ds(dst, RT)],
                                  sem.at[slot]).start(priority=priority)

        for e in range(N_EXPERTS):
            pe = s * N_EXPERTS + e
            lo = off_ref[pe * LANES + b]
            n = off_ref[pe * LANES + b + 1] - lo
            base = pe * CAP + lo
            nq = lax.shift_right_logical(n, COMBINE_UNROLL.bit_length() - 1)

            def issue_many(q, c, base=base):
                for u in range(COMBINE_UNROLL):
                    fetch(base + q * COMBINE_UNROLL + u, priority=u % 2)
                return c
            lax.fori_loop(0, nq, issue_many, 0)

            def issue_one(r, c):
                fetch(r)
                return c
            lax.fori_loop(base + nq * COMBINE_UNROLL, base + n, issue_one, 0)

    def rows_of_tile(tile):
        s = tile // NTT
        b = tile % NTT
        total = 0
        for e in range(N_EXPERTS):
            pe = s * N_EXPERTS + e
            total = total + off_ref[pe * LANES + b + 1] - off_ref[pe * LANES + b]
        return total

    @pl.when(i == 0)
    def _():
        fetch_tile(i, buf)

    @pl.when(i + 1 < ntile)
    def _():
        fetch_tile(i + 1, 1 - buf)

    total = rows_of_tile(i)

    @pl.when(total > 0)
    def _():
        pltpu.make_async_copy(eo_ref.at[pl.ds(0, total * RT)], stage_ref.at[pl.ds(0, total * RT)],
                              sem.at[buf]).wait()

    cnt = cnt_ref[:, 0:1]
    o_ref[...] = jnp.zeros_like(o_ref)

    def add(k, c):
        rows = _load_row_tiles(stage_ref, pl.multiple_of(buf * STAGE_ROWS + k * TT, TT), TT)
        o_ref[...] += jnp.where(cnt > k.astype(F32), rows, 0.0)
        return c
    lax.fori_loop(0, maxk_ref[(i // NTT) * LANES + i % NTT], add, 0)
    x2 = x1_ref[...] + mod_ref[0, 5:6, :] * o_ref[...]
    if final:
        x2 = _rms(x2, gf_ref[...])
    o_ref[...] = x2

    @pl.when(i < NTT)
    def _():
        octx_ref[...] = x2


def _combine(offs, dst_rows, maxk, x1, mod_l, cnt_cols, g_final, eo, final):
    def mrow(i):
        return jnp.where(i < NTT, 0, 1 + (i - NTT) // (LAT_L // TT))
    return pl.pallas_call(
        functools.partial(_combine_kernel, final=final),
        grid_spec=pltpu.PrefetchScalarGridSpec(
            num_scalar_prefetch=3,
            grid=(NALL // TT,),
            in_specs=[
                pl.BlockSpec((TT, D), lambda i, *_: (i, 0)),
                pl.BlockSpec((1, 8, D), lambda i, *_: (mrow(i), 0, 0)),
                pl.BlockSpec((TT, LANES), lambda i, *_: (i, 0)),
                pl.BlockSpec((1, D), lambda i, *_: (0, 0)),
                pl.BlockSpec(memory_space=pl.ANY),
            ],
            out_specs=_stream_specs(TT),
            scratch_shapes=[pltpu.VMEM((2 * STAGE_ROWS * RT, LANES), F32), pltpu.SemaphoreType.DMA((2,))],
        ),
        out_shape=[jax.ShapeDtypeStruct((NTOK, D), F32)] * 2,
        compiler_params=_cp(("arbitrary",)),
        name="combine",
    )(offs.reshape(-1), dst_rows, maxk.reshape(-1), x1, mod_l, cnt_cols, g_final.reshape(1, D), eo)


def _gate_blocks(w_a, w_x):
    hd = w_a.shape[-1]
    per = LANES // hd
    ncb = D_LRU // LANES
    mats = jnp.stack([w_a[0], w_x[0], w_a[1], w_x[1]], axis=0)
    mats = mats.reshape(4, ncb, per, hd, hd)
    out = jnp.zeros((ncb, 4, LANES, LANES), F32)
    for q in range(per):
        out = out.at[:, :, q * hd:(q + 1) * hd, q * hd:(q + 1) * hd].set(mats[:, :, q].transpose(1, 0, 2, 3))
    return out.astype(MM)


def kernel(x_prompt, x_sample, state_rglru, c, c_ctx, w_mod, b_mod, g_norm_mix, g_norm_ffn, w_in, w_pool,
           pool_scale, w_conv, b_conv, w_gate_a, b_gate_a, w_gate_x, b_gate_x, lru_lambda, w_out, w_router,
           w_exp_gate, w_exp_up, w_exp_down, g_final):
    nb_ctx = x_prompt.shape[0]
    nb_lat = x_sample.shape[0]
    xa, xb = x_prompt.reshape(NTOK, D), x_sample.reshape(NTOK, D)

    cvec = jnp.zeros((8, D), F32).at[0].set(c_ctx).at[1:1 + nb_lat].set(c)
    mod = _modulation(cvec, w_mod, b_mod)
    mod = jnp.pad(mod.transpose(0, 2, 1, 3), ((0, 0), (0, 0), (0, 2), (0, 0)))

    states = []
    for l in range(DEPTH):
        u_pool, u_lru, u_gate = _inproj(xa, xb, mod[l], g_norm_mix[l], w_in[l].astype(MM))
        y_pool = _pool(u_pool, w_pool[l].astype(MM), pool_scale[l])
        bg = jnp.stack([b_gate_a[l, 0], b_gate_x[l, 0], b_gate_a[l, 1], b_gate_x[l, 1]], axis=0).reshape(4, D_LRU)
        y_lru, st = _lru(l, u_lru, u_gate, w_conv[l], b_conv[l], 0.5 * _gate_blocks(w_gate_a[l], w_gate_x[l]),
                         0.5 * bg, lru_lambda[l], state_rglru)
        states.append(st[:NTOK // SEG].transpose(0, 2, 1, 3).reshape(nb_ctx, 2, D_LRU))

        wr = w_router[l].T
        wr_hi = wr.astype(BF16)
        wr_split = jnp.stack([wr_hi, (wr - wr_hi.astype(F32)).astype(BF16)], axis=0)
        x1, hn, aff_t = _outproj(xa, xb, y_pool, y_lru, mod[l], g_norm_ffn[l], w_out[l].astype(MM), wr_split)
        slot, kord, cnt, offs, maxk = _plan(aff_t)
        aff_rows = aff_t.reshape(N_EXPERTS, NALL // NTOK, NTOK).transpose(1, 0, 2)
        lists = _compact(offs, slot, aff_rows, kord)
        tok_local = (lists[:, 0] * TT + lists[:, 1]).astype(jnp.int32)
        g_sel = lists[:, 2] + lists[:, 3] + lists[:, 4]
        kord_i = lists[:, 5].astype(jnp.int32)
        stream_base = (jnp.arange(NPAIR, dtype=jnp.int32) // N_EXPERTS) * NTOK
        src_rows = (tok_local + stream_base[:, None]).reshape(-1)
        g_cols = jnp.broadcast_to(g_sel[:, :, None], g_sel.shape + (LANES,))
        eo = _ffn(l, src_rows, hn, w_exp_gate, w_exp_up, w_exp_down, g_cols)
        cnt_cols = jnp.broadcast_to(cnt.reshape(NALL, 1), (NALL, LANES))
        dst_rows = (kord_i * TT + tok_local % TT).reshape(-1)
        xa, xb = _combine(offs, dst_rows, maxk, x1, mod[l], cnt_cols, g_final, eo, final=(l == DEPTH - 1))

    return xa.reshape(x_prompt.shape), xb.reshape(x_sample.shape), jnp.stack(states, axis=1)
```
